```python
import math
import jax
import jax.numpy as jnp
from jax import lax
import numpy as np


D_MODEL = 2048
BATCH = 2
SEQ = 8192
DEPTH = 4

CHUNK = 64
Q_BLOCK = 128
SB_HEADS = 6
SB_HEAD_DIM = 128
SB_WIDTH = SB_HEADS * SB_HEAD_DIM
SSM_WIDTH = D_MODEL // 4
SSM_GROUP = 16
SSM_GROUPS = SSM_WIDTH // SSM_GROUP
SSM_STATE = 64
DSA_HEADS = 6
DSA_HEAD_DIM = 128
DSA_WIDTH = DSA_HEADS * DSA_HEAD_DIM
IDX_HEADS = 16
IDX_DIM = 64
DSA_MAX_TOPK = 256
REL_BUCKETS = 32
REL_MAX_DIST = 128
D_MIX = SB_WIDTH + SSM_WIDTH + DSA_WIDTH
D_FF = 5504
CONV_WIDTH = 3
N_MOD = 6
LN_EPS = 1e-5
DEEPNORM_ALPHA = (2 * DEPTH) ** 0.25
DEEPNORM_BETA = (8 * DEPTH) ** -0.25
IN_SIZES = (SB_WIDTH, SB_WIDTH, SB_WIDTH, SSM_WIDTH, DSA_WIDTH, DSA_WIDTH, DSA_WIDTH,
            IDX_HEADS * IDX_DIM, IDX_DIM, IDX_HEADS)
IN_COLS = sum(IN_SIZES)
IN_SPLIT_POINTS = tuple(int(s) for s in np.cumsum(IN_SIZES)[:-1])

kernel_name = "hybrid_sb_s5_dsa_deepnorm_adaln_trunk"


def layer_norm(x, g, b):
    xf = x.astype(jnp.float32)
    mu = jnp.mean(xf, axis=-1, keepdims=True)
    var = jnp.mean(jnp.square(xf - mu), axis=-1, keepdims=True)
    return ((xf - mu) * lax.rsqrt(var + LN_EPS) * g + b).astype(x.dtype)


def to_blocks(a):
    B, T = a.shape[0], a.shape[1]
    return jnp.moveaxis(a.reshape((B, T // Q_BLOCK, Q_BLOCK) + a.shape[2:]), 1, 0)


def from_blocks(a):
    a = jnp.moveaxis(a, 0, 1)
    return a.reshape((a.shape[0], a.shape[1] * a.shape[2]) + a.shape[3:])


def t5_bucket(rel):
    half = REL_BUCKETS // 2
    max_exact = half // 2
    n = jnp.abs(rel)
    nf = jnp.maximum(n, 1).astype(jnp.float32)
    large = max_exact + (jnp.log(nf / max_exact) / math.log(REL_MAX_DIST / max_exact)
                         * (half - max_exact)).astype(jnp.int32)
    large = jnp.minimum(large, half - 1)
    return jnp.where(rel > 0, half, 0) + jnp.where(n < max_exact, n, large)


def stick_breaking_attention(q, k, v):
    T, dh = q.shape[1], q.shape[3]
    scale = dh ** -0.5
    key_pos = jnp.arange(T)

    def block(args):
        qi, blk = args
        q_pos = blk * Q_BLOCK + jnp.arange(Q_BLOCK)
        z = jnp.einsum('bqhd,bkhd->bhqk', qi, k).astype(jnp.float32) * scale
        mask = key_pos[None, :] < q_pos[:, None]
        log_1m = jnp.where(mask, jax.nn.log_sigmoid(-z), 0.0)
        suffix = lax.cumsum(log_1m, axis=3, reverse=True) - log_1m
        w = jnp.where(mask, jnp.exp(jax.nn.log_sigmoid(z) + suffix), 0.0)
        return jnp.einsum('bhqk,bkhd->bqhd', w.astype(v.dtype), v)

    out = lax.map(block, (to_blocks(q), jnp.arange(T // Q_BLOCK)))
    return from_blocks(out)


def dsa_attention(q, k, v, q_idx, k_idx, w_idx, rel_bias):
    T, dh = q.shape[1], q.shape[3]
    topk = min(DSA_MAX_TOPK, T // 4)
    scale = dh ** -0.5
    key_chunk = jnp.arange(T) // CHUNK

    def block(args):
        qi, qii, wi, blk = args
        q_pos = blk * Q_BLOCK + jnp.arange(Q_BLOCK)
        q_chunk = q_pos // CHUNK
        admissible = key_chunk[None, :] <= q_chunk[:, None]
        rel_scores = jax.nn.relu(jnp.einsum('bqhd,bkd->bqhk', qii, k_idx).astype(jnp.float32)
                                 * (IDX_DIM ** -0.5))
        score = jnp.einsum('bqh,bqhk->bqk', wi.astype(jnp.float32) * (IDX_HEADS ** -0.5),
                           rel_scores)
        score = jnp.where(admissible[None], score, -jnp.inf)
        _, sel = lax.top_k(score, topk)
        valid = (sel // CHUNK) <= q_chunk[None, :, None]
        k_sel = jax.vmap(lambda kb, ib: kb[ib])(k, sel)
        v_sel = jax.vmap(lambda vb, ib: vb[ib])(v, sel)
        logits = jnp.einsum('bqhd,bqkhd->bhqk', qi, k_sel).astype(jnp.float32) * scale
        bias = rel_bias[t5_bucket(sel - q_pos[None, :, None])]
        logits = logits + jnp.transpose(bias, (0, 3, 1, 2)).astype(jnp.float32)
        logits = jnp.where(valid[:, None], logits, -1e30)
        p = jax.nn.softmax(logits, axis=-1)
        return jnp.einsum('bhqk,bqkhd->bqhd', p.astype(v.dtype), v_sel)

    out = lax.map(block, (to_blocks(q), to_blocks(q_idx), to_blocks(w_idx),
                          jnp.arange(T // Q_BLOCK)))
    return from_blocks(out)


def s5_mixer(u, a_re, a_im, log_dt, b_re, b_im, c_re, c_im, d_skip, w_glu, b_glu):
    B, T, _ = u.shape
    f32 = jnp.float32
    ug = u.reshape(B, T, SSM_GROUPS, SSM_GROUP).astype(f32)
    ar, ai = a_re.astype(f32), a_im.astype(f32)
    dt = jnp.exp(log_dt.astype(f32))[:, None]
    mag = jnp.exp(dt * ar)
    abar_re, abar_im = mag * jnp.cos(dt * ai), mag * jnp.sin(dt * ai)
    den = ar * ar + ai * ai
    nr = abar_re - 1.0
    f_re = (nr * ar + abar_im * ai) / den
    f_im = (abar_im * ar - nr * ai) / den
    br, bi = b_re.astype(f32), b_im.astype(f32)
    bb_re = f_re[..., None] * br - f_im[..., None] * bi
    bb_im = f_re[..., None] * bi + f_im[..., None] * br
    bu_re = jnp.einsum('gph,btgh->btgp', bb_re, ug)
    bu_im = jnp.einsum('gph,btgh->btgp', bb_im, ug)
    a_re_t = jnp.broadcast_to(abar_re, bu_re.shape)
    a_im_t = jnp.broadcast_to(abar_im, bu_im.shape)

    def combine(e1, e2):
        a1r, a1i, b1r, b1i = e1
        a2r, a2i, b2r, b2i = e2
        return (a2r * a1r - a2i * a1i, a2r * a1i + a2i * a1r,
                a2r * b1r - a2i * b1i + b2r, a2r * b1i + a2i * b1r + b2i)

    _, _, h_re, h_im = lax.associative_scan(combine, (a_re_t, a_im_t, bu_re, bu_im), axis=1)
    y = (jnp.einsum('ghp,btgp->btgh', c_re.astype(f32), h_re)
         - jnp.einsum('ghp,btgp->btgh', c_im.astype(f32), h_im)
         + d_skip.astype(f32) * ug)
    y = jax.nn.gelu(y.reshape(B, T, SSM_WIDTH))
    y = y * jax.nn.sigmoid(y @ w_glu.astype(f32) + b_glu.astype(f32))
    return y.astype(u.dtype)


def causal_depthwise_conv(a, w, b):
    C = a.shape[-1]
    out = lax.conv_general_dilated(a, w[:, None, :].astype(a.dtype), window_strides=(1,),
                                   padding=[(CONV_WIDTH - 1, 0)],
                                   dimension_numbers=('NWC', 'WIO', 'NWC'),
                                   feature_group_count=C)
    return out + b


def conv_ffn(h, w_up, conv_w, conv_b, w_down):
    a = causal_depthwise_conv(h @ w_up, conv_w, conv_b)
    g, val = jnp.split(a, 2, axis=-1)
    return (jax.nn.silu(g) * val) @ w_down


def setup_inputs(seed: int = 0) -> dict:
    key = jax.random.key(seed)
    ks = jax.random.split(key, 25)
    f32 = jnp.float32
    L, G, P, Hc = DEPTH, SSM_GROUPS, SSM_STATE, SSM_GROUP

    def nrm(k, shape, s):
        return jax.random.normal(k, shape, f32) * s

    n_idx = jnp.arange(P, dtype=f32)
    return {
        "x": nrm(ks[0], (BATCH, SEQ, D_MODEL), 1.0),
        "c": nrm(ks[1], (BATCH, D_MODEL), 1.0),
        "w_ada": nrm(ks[2], (L, D_MODEL, N_MOD * D_MODEL), 0.5 * D_MODEL ** -0.5),
        "b_ada": nrm(ks[3], (L, N_MOD * D_MODEL), 0.01),
        "w_in": nrm(ks[4], (L, D_MODEL, IN_COLS), D_MODEL ** -0.5),
        "w_out": nrm(ks[5], (L, D_MIX, D_MODEL), DEEPNORM_BETA * D_MIX ** -0.5),
        "ssm_a_re": -0.5 + nrm(ks[6], (L, G, P), 0.01),
        "ssm_a_im": math.pi * n_idx + nrm(ks[7], (L, G, P), 0.01),
        "ssm_log_dt": jax.random.uniform(ks[8], (L, G), f32, math.log(1e-3), math.log(1e-1)),
        "ssm_b_re": nrm(ks[9], (L, G, P, Hc), (2 * Hc) ** -0.5),
        "ssm_b_im": nrm(ks[10], (L, G, P, Hc), (2 * Hc) ** -0.5),
        "ssm_c_re": nrm(ks[11], (L, G, Hc, P), 0.5 ** 0.5),
        "ssm_c_im": nrm(ks[12], (L, G, Hc, P), 0.5 ** 0.5),
        "ssm_d": nrm(ks[13], (L, G, Hc), 1.0),
        "w_glu": nrm(ks[14], (L, SSM_WIDTH, SSM_WIDTH), SSM_WIDTH ** -0.5),
        "b_glu": nrm(ks[15], (L, SSM_WIDTH), 0.01),
        "rel_bias": nrm(ks[16], (REL_BUCKETS, DSA_HEADS), 0.2),
        "ln1_g": 1.0 + nrm(ks[17], (L, D_MODEL), 0.02),
        "ln1_b": nrm(ks[18], (L, D_MODEL), 0.02),
        "w_up": nrm(ks[19], (L, D_MODEL, 2 * D_FF), D_MODEL ** -0.5),
        "conv_w": nrm(ks[20], (L, CONV_WIDTH, 2 * D_FF), CONV_WIDTH ** -0.5),
        "conv_b": nrm(ks[21], (L, 2 * D_FF), 0.01),
        "w_down": nrm(ks[22], (L, D_FF, D_MODEL), DEEPNORM_BETA * D_FF ** -0.5),
        "ln2_g": 1.0 + nrm(ks[23], (L, D_MODEL), 0.02),
        "ln2_b": nrm(ks[24], (L, D_MODEL), 0.02),
    }


def reference(x, c, w_ada, b_ada, w_in, w_out, ssm_a_re, ssm_a_im, ssm_log_dt, ssm_b_re,
              ssm_b_im, ssm_c_re, ssm_c_im, ssm_d, w_glu, b_glu, rel_bias, ln1_g, ln1_b,
              w_up, conv_w, conv_b, w_down, ln2_g, ln2_b):
    B, T, _ = x.shape
    cond = jax.nn.silu(c)
    for l in range(DEPTH):
        mod = (cond @ w_ada[l] + b_ada[l])[:, None, :]
        sh_m, sc_m, g_m, sh_f, sc_f, g_f = jnp.split(mod, N_MOD, axis=-1)
        u = x * (1.0 + sc_m) + sh_m
        sb_q, sb_k, sb_v, ssm_u, dq, dk, dv, iq, ik, iw = jnp.split(
            u @ w_in[l], IN_SPLIT_POINTS, axis=-1)
        sb_o = stick_breaking_attention(sb_q.reshape(B, T, SB_HEADS, SB_HEAD_DIM),
                                        sb_k.reshape(B, T, SB_HEADS, SB_HEAD_DIM),
                                        sb_v.reshape(B, T, SB_HEADS, SB_HEAD_DIM))
        ssm_o = s5_mixer(ssm_u, ssm_a_re[l], ssm_a_im[l], ssm_log_dt[l], ssm_b_re[l],
                         ssm_b_im[l], ssm_c_re[l], ssm_c_im[l], ssm_d[l], w_glu[l], b_glu[l])
        dsa_o = dsa_attention(dq.reshape(B, T, DSA_HEADS, DSA_HEAD_DIM),
                              dk.reshape(B, T, DSA_HEADS, DSA_HEAD_DIM),
                              dv.reshape(B, T, DSA_HEADS, DSA_HEAD_DIM),
                              iq.reshape(B, T, IDX_HEADS, IDX_DIM), ik, iw, rel_bias)
        mix = jnp.concatenate([sb_o.reshape(B, T, SB_WIDTH), ssm_o,
                               dsa_o.reshape(B, T, DSA_WIDTH)], axis=-1) @ w_out[l]
        x = layer_norm(DEEPNORM_ALPHA * x + (1.0 + g_m) * mix, ln1_g[l], ln1_b[l])
        u = x * (1.0 + sc_f) + sh_f
        ff = conv_ffn(u, w_up[l], conv_w[l], conv_b[l], w_down[l])
        x = layer_norm(DEEPNORM_ALPHA * x + (1.0 + g_f) * ff, ln2_g[l], ln2_b[l])
    return x
```

```python
import functools
import math

import jax
import jax.numpy as jnp
from jax import lax
from jax.experimental import pallas as pl
from jax.experimental.pallas import tpu as pltpu

F32 = jnp.float32
BF16 = jnp.bfloat16
I32 = jnp.int32

CHUNK = 64
CHUNK_SHIFT = CHUNK.bit_length() - 1
assert 1 << CHUNK_SHIFT == CHUNK
SB_HEADS = 6
DSA_HEADS = 6
HEAD_DIM = 128
SB_WIDTH = SB_HEADS * HEAD_DIM
DSA_WIDTH = DSA_HEADS * HEAD_DIM
SSM_GROUP = 16
SSM_STATE = 64
IDX_HEADS = 16
IDX_DIM = 64
DSA_MAX_TOPK = 256
REL_BUCKETS = 32
REL_MAX_DIST = 128
CONV_WIDTH = 3
N_MOD = 6
LN_EPS = 1e-5

LANE = 128
VMEM_LIMIT_BYTES = 56 * 1024 * 1024

SSM_CHUNK = 16
SB_BLOCK = 256
SB_ZERO_LOG = 104.0
DSA_TQ = 128
DSA_TK = 256
KEY_NEG_INF = -2139095041
INT_MIN = -2147483648
MASKED_LOGIT = -1e30


def _params(n_axes):
    return pltpu.CompilerParams(dimension_semantics=("arbitrary",) * n_axes,
                                vmem_limit_bytes=VMEM_LIMIT_BYTES)


def _dot(a, b):
    return jnp.dot(a, b, preferred_element_type=F32)


def _dot_nt(a, b):
    return lax.dot_general(a, b, (((1,), (1,)), ((), ())), preferred_element_type=F32)


def _split_bf16(a):
    hi = a.astype(BF16)
    lo = (a - hi.astype(F32)).astype(BF16)
    return hi, lo


def _dot3(a, b):
    ah, al = _split_bf16(a)
    bh, bl = _split_bf16(b)
    return _dot(ah, bh) + _dot(ah, bl) + _dot(al, bh)


def _layer_norm(y, g, b):
    mu = jnp.mean(y, axis=-1, keepdims=True)
    d = y - mu
    var = jnp.mean(d * d, axis=-1, keepdims=True)
    return d * lax.rsqrt(var + LN_EPS) * g + b


def _mod_kernel(c_ref, w_ref, b_ref, o_ref):
    c = c_ref[...]
    cond = c * jax.nn.sigmoid(c)
    o_ref[...] = _dot(cond.astype(BF16), w_ref[...].astype(BF16)) + b_ref[...]


def _adaln_mod(c, w_ada, b_ada):
    depth, d_model, n_out = w_ada.shape
    rows = 8
    c_pad = jnp.zeros((rows, d_model), F32).at[: c.shape[0]].set(c)
    tn = 1024
    return pl.pallas_call(
        _mod_kernel,
        grid=(depth, n_out // tn),
        in_specs=[
            pl.BlockSpec((rows, d_model), lambda l, j: (0, 0)),
            pl.BlockSpec((None, d_model, tn), lambda l, j: (l, 0, j)),
            pl.BlockSpec((None, 1, tn), lambda l, j: (l, 0, j)),
        ],
        out_specs=pl.BlockSpec((None, rows, tn), lambda l, j: (l, 0, j)),
        out_shape=jax.ShapeDtypeStruct((depth, rows, n_out), F32),
        compiler_params=_params(2),
        name="adaln_mod",
    )(c_pad, w_ada, b_ada.reshape(depth, 1, n_out))


def _inproj_kernel(x_ref, sc_ref, sh_ref, w_ref, o_ref, u_ref):
    @pl.when(pl.program_id(1) == 0)
    def _():
        u_ref[...] = (x_ref[...] * (1.0 + sc_ref[...]) + sh_ref[...]).astype(BF16)

    o_ref[...] = _dot(u_ref[...], w_ref[...]).astype(o_ref.dtype)


def _modulated_matmul(x, scale, shift, w, out_dtype, seq_len, tm, tn, name):
    n, d = x.shape
    ncols = w.shape[1]
    return pl.pallas_call(
        _inproj_kernel,
        grid=(n // tm, ncols // tn),
        in_specs=[
            pl.BlockSpec((tm, d), lambda i, j: (i, 0)),
            pl.BlockSpec((None, 1, d), lambda i, j: ((i * tm) // seq_len, 0, 0)),
            pl.BlockSpec((None, 1, d), lambda i, j: ((i * tm) // seq_len, 0, 0)),
            pl.BlockSpec((d, tn), lambda i, j: (0, j)),
        ],
        out_specs=pl.BlockSpec((tm, tn), lambda i, j: (i, j)),
        out_shape=jax.ShapeDtypeStruct((n, ncols), out_dtype),
        scratch_shapes=[pltpu.VMEM((tm, d), BF16)],
        compiler_params=_params(2),
        name=name,
    )(x, scale, shift, w)


def _sb_kernel(q_ref, k_ref, v_ref, o_ref, acc_ref, run_ref, *, blk, scale):
    i = pl.program_id(2)
    q = q_ref[...]
    rows = lax.broadcasted_iota(I32, (blk, blk), 0)
    cols = lax.broadcasted_iota(I32, (blk, blk), 1)
    tri = jnp.where(rows > cols, 1.0, 0.0).astype(BF16)
    keep = cols < rows

    def block(kb, masked):
        start = pl.multiple_of(kb * blk, blk)
        k = k_ref[pl.ds(start, blk), :]
        v = v_ref[pl.ds(start, blk), :]
        z = _dot_nt(q, k) * scale
        softplus = jnp.maximum(z, 0.0) + jnp.log(1.0 + jnp.exp(-jnp.abs(z)))
        log_1m = -softplus
        if masked:
            log_1m = jnp.where(keep, log_1m, 0.0)
        hi, lo = _split_bf16(log_1m)
        suffix = _dot(hi, tri) + _dot(lo, tri)
        run = run_ref[...]
        w = jnp.exp((z - softplus) + suffix + run)
        if masked:
            w = jnp.where(keep, w, 0.0)
        acc_ref[...] += _dot(w.astype(BF16), v)
        run_new = run + jnp.sum(log_1m, axis=1, keepdims=True)
        run_ref[...] = run_new
        return jnp.max(run_new)

    acc_ref[...] = jnp.zeros_like(acc_ref)
    run_ref[...] = jnp.zeros_like(run_ref)
    top = block(i, True)

    def cond(carry):
        kb, top = carry
        return jnp.logical_and(kb >= 0, top > -SB_ZERO_LOG)

    def body(carry):
        kb, _ = carry
        return kb - 1, block(kb, False)

    lax.while_loop(cond, body, (i - 1, top))
    o_ref[...] = acc_ref[...].astype(o_ref.dtype)


def _sb_attention(proj, batch, seq_len, q_col, k_col, v_col):
    n = proj.shape[0]
    blk = min(SB_BLOCK, seq_len)
    nq = seq_len // blk
    kern = functools.partial(_sb_kernel, blk=blk, scale=HEAD_DIM ** -0.5)
    return pl.pallas_call(
        kern,
        grid=(batch, SB_HEADS, nq),
        in_specs=[
            pl.BlockSpec((blk, HEAD_DIM), lambda b, h, i: (b * nq + i, q_col + h)),
            pl.BlockSpec((seq_len, HEAD_DIM), lambda b, h, i: (b, k_col + h)),
            pl.BlockSpec((seq_len, HEAD_DIM), lambda b, h, i: (b, v_col + h)),
        ],
        out_specs=pl.BlockSpec((blk, HEAD_DIM), lambda b, h, i: (b * nq + i, h)),
        out_shape=jax.ShapeDtypeStruct((n, SB_WIDTH), BF16),
        scratch_shapes=[pltpu.VMEM((blk, HEAD_DIM), F32), pltpu.VMEM((blk, 1), F32)],
        compiler_params=_params(3),
        name="sb_attention",
    )(proj, proj, proj)


def _t5_bucket(rel):
    half = REL_BUCKETS // 2
    max_exact = half // 2
    n = jnp.abs(rel)
    nf = jnp.maximum(n, 1).astype(F32)
    large = max_exact + (jnp.log(nf / max_exact) / math.log(REL_MAX_DIST / max_exact)
                         * (half - max_exact)).astype(I32)
    large = jnp.minimum(large, half - 1)
    return jnp.where(rel > 0, half, 0) + jnp.where(n < max_exact, n, large)


def _dsa_kernel(rb_ref, qd_ref, kd_ref, vd_ref, iq_ref, ika_ref, ikb_ref, iw_ref, o_ref,
                ka_s, kb_s, wb_s, keys_s, bias_s, acc_s, m_s, l_s, j_s,
                *, tq, tk, topk, seq_len, scale):
    i = pl.program_id(1)
    q0 = i * tq
    n_kb = (q0 + tq + tk - 1) // tk
    bias_w = bias_s.shape[2]
    bias_shift = bias_w - tk

    @pl.when(i == 0)
    def _setup():
        ka_s[...] = ika_ref[...].astype(BF16)
        kb_s[...] = ikb_ref[...].astype(BF16)
        r = lax.broadcasted_iota(I32, (tq, bias_w), 0)
        c = lax.broadcasted_iota(I32, (tq, bias_w), 1)
        bucket = _t5_bucket(c - r - bias_shift)
        for h in range(DSA_HEADS):
            bias = jnp.zeros((tq, bias_w), F32)
            for j in range(REL_BUCKETS):
                bias = jnp.where(bucket == j, rb_ref[j, h], bias)
            bias_s[h] = bias

    iw = iw_ref[...]
    for h in range(IDX_HEADS):
        wb_s[h] = jnp.broadcast_to(iw[:, h:h + 1] * (IDX_DIM ** -0.5 * IDX_HEADS ** -0.5), (tq, LANE))

    rows = lax.broadcasted_iota(I32, (tq, tk), 0)
    cols = lax.broadcasted_iota(I32, (tq, tk), 1)
    q_chunk = (q0 + rows) >> CHUNK_SHIFT

    def score_block(kb, carry):
        start = pl.multiple_of(kb * tk, tk)
        k_even = ka_s[pl.ds(start, tk), :]
        k_odd = kb_s[pl.ds(start, tk), :]
        halves = [jnp.zeros((tq, LANE), F32) for _ in range(tk // LANE)]
        for pair in range(IDX_HEADS // 2):
            lhs = iq_ref[:, pair * LANE:(pair + 1) * LANE]
            for kk, h in ((k_even, 2 * pair), (k_odd, 2 * pair + 1)):
                d = _dot_nt(lhs, kk)
                w = wb_s[h]
                for s in range(tk // LANE):
                    halves[s] = halves[s] + jnp.maximum(d[:, s * LANE:(s + 1) * LANE], 0.0) * w
        score = jnp.concatenate(halves, axis=1)
        score = jnp.where(score == 0.0, 0.0, score)
        admissible = ((start + cols) >> CHUNK_SHIFT) <= q_chunk
        score = jnp.where(admissible, score, -jnp.inf)
        bits = pltpu.bitcast(score, I32)
        keys_s[:, pl.ds(start, tk)] = jnp.where(bits < 0, bits ^ 0x7FFFFFFF, bits)
        return carry

    lax.fori_loop(0, n_kb, score_block, 0)

    def count(pred):
        def body(kb, acc):
            start = pl.multiple_of(kb * tk, tk)
            hit = pred(keys_s[:, pl.ds(start, tk)], start)
            for s in range(tk // LANE):
                acc = acc + jnp.where(hit[:, s * LANE:(s + 1) * LANE], 1.0, 0.0)
            return acc
        acc = lax.fori_loop(0, n_kb, body, jnp.zeros((tq, LANE), F32))
        return jnp.sum(acc, axis=1, keepdims=True)

    def select_bit(it, prefix):
        bit = 31 - it
        cand = jnp.where(bit == 31, jnp.zeros_like(prefix), prefix | jnp.left_shift(1, bit))
        cnt = count(lambda k, start: k >= cand)
        return jnp.where(cnt >= topk, cand, prefix)

    kth = lax.fori_loop(0, 32, select_bit, jnp.full((tq, 1), INT_MIN, I32))

    n_above = count(lambda k, start: k > kth)
    n_at_least = count(lambda k, start: k >= kth)
    need = topk - n_above
    tied = jnp.logical_and(n_at_least > topk, kth > KEY_NEG_INF)
    j_s[...] = jnp.full((tq, 1), seq_len, I32)

    @pl.when(jnp.max(jnp.where(tied, 1.0, 0.0)) > 0.0)
    def _break_ties():
        def pos_bit(it, last):
            bit = (seq_len - 1).bit_length() - 1 - it
            cand = last | jnp.left_shift(1, bit)
            cnt = count(lambda k, start: jnp.where((start + cols) < cand, k, INT_MIN) == kth)
            return jnp.where(cnt < need, cand, last)
        j_s[...] = lax.fori_loop(0, (seq_len - 1).bit_length(), pos_bit, jnp.zeros((tq, 1), I32))

    last_tied = j_s[...]
    kth_eff = jnp.maximum(kth, KEY_NEG_INF + 1)

    m_s[...] = jnp.full_like(m_s, MASKED_LOGIT)
    l_s[...] = jnp.zeros_like(l_s)
    acc_s[...] = jnp.zeros_like(acc_s)

    def attend_block(kb, carry):
        start = pl.multiple_of(kb * tk, tk)
        keys = keys_s[:, pl.ds(start, tk)]
        sel = keys >= jnp.where((start + cols) > last_tied, kth_eff + 1, kth_eff)
        off = pl.multiple_of(jnp.maximum(start - q0 + bias_shift, 0), LANE)
        for h in range(DSA_HEADS):
            hs = slice(h * HEAD_DIM, (h + 1) * HEAD_DIM)
            s = _dot_nt(qd_ref[:, hs], kd_ref[pl.ds(start, tk), hs]) * scale
            s = s + bias_s[h, :, pl.ds(off, tk)]
            s = jnp.where(sel, s, MASKED_LOGIT)
            m_old = m_s[h]
            m_new = jnp.maximum(m_old, jnp.max(s, axis=1, keepdims=True))
            p = jnp.where(sel, jnp.exp(s - m_new), 0.0)
            alpha = jnp.exp(m_old - m_new)
            l_s[h] = alpha * l_s[h] + jnp.sum(p, axis=1, keepdims=True)
            acc_s[h] = alpha * acc_s[h] + _dot(p.astype(BF16), vd_ref[pl.ds(start, tk), hs])
            m_s[h] = m_new
        return carry

    lax.fori_loop(0, n_kb, attend_block, 0)
    for h in range(DSA_HEADS):
        o_ref[:, h * HEAD_DIM:(h + 1) * HEAD_DIM] = (acc_s[h] / l_s[h]).astype(o_ref.dtype)


def _dsa_attention(proj, side, rel_bias, batch, seq_len, cols):
    n = proj.shape[0]
    tq = min(DSA_TQ, seq_len)
    tk = min(DSA_TK, seq_len)
    nq = seq_len // tq
    topk = min(DSA_MAX_TOPK, seq_len // 4)
    bias_w = tk + 2 * tk
    kern = functools.partial(_dsa_kernel, tq=tq, tk=tk, topk=topk, seq_len=seq_len,
                             scale=HEAD_DIM ** -0.5)
    once = pl.Buffered(1)
    return pl.pallas_call(
        kern,
        grid=(batch, nq),
        in_specs=[
            pl.BlockSpec(memory_space=pltpu.SMEM),
            pl.BlockSpec((tq, DSA_WIDTH), lambda b, i: (b * nq + i, cols["dq"])),
            pl.BlockSpec((seq_len, DSA_WIDTH), lambda b, i: (b, cols["dk"]), pipeline_mode=once),
            pl.BlockSpec((seq_len, DSA_WIDTH), lambda b, i: (b, cols["dv"]), pipeline_mode=once),
            pl.BlockSpec((tq, IDX_HEADS * IDX_DIM), lambda b, i: (b * nq + i, cols["iq"])),
            pl.BlockSpec((seq_len, LANE), lambda b, i: (b, cols["ik_even"]), pipeline_mode=once),
            pl.BlockSpec((seq_len, LANE), lambda b, i: (b, cols["ik_odd"]), pipeline_mode=once),
            pl.BlockSpec((tq, LANE), lambda b, i: (b * nq + i, cols["iw"])),
        ],
        out_specs=pl.BlockSpec((tq, DSA_WIDTH), lambda b, i: (b * nq + i, 0)),
        out_shape=jax.ShapeDtypeStruct((n, DSA_WIDTH), BF16),
        scratch_shapes=[
            pltpu.VMEM((seq_len, LANE), BF16),
            pltpu.VMEM((seq_len, LANE), BF16),
            pltpu.VMEM((IDX_HEADS, tq, LANE), F32),
            pltpu.VMEM((tq, seq_len), I32),
            pltpu.VMEM((DSA_HEADS, tq, bias_w), F32),
            pltpu.VMEM((DSA_HEADS, tq, HEAD_DIM), F32),
            pltpu.VMEM((DSA_HEADS, tq, 1), F32),
            pltpu.VMEM((DSA_HEADS, tq, 1), F32),
            pltpu.VMEM((tq, 1), I32),
        ],
        compiler_params=_params(2),
        name="dsa_attention",
    )(rel_bias, proj, proj, proj, proj, side, side, side)


def _s5_operators(a_re, a_im, log_dt, b_re, b_im, c_re, c_im):
    hp = lax.Precision.HIGHEST
    L = SSM_CHUNK
    ar, ai = a_re.astype(F32), a_im.astype(F32)
    dt = jnp.exp(log_dt.astype(F32))[:, None]
    mag = jnp.exp(dt * ar)
    abar_re, abar_im = mag * jnp.cos(dt * ai), mag * jnp.sin(dt * ai)
    den = ar * ar + ai * ai
    nr = abar_re - 1.0
    f_re = (nr * ar + abar_im * ai) / den
    f_im = (abar_im * ar - nr * ai) / den
    br, bi = b_re.astype(F32), b_im.astype(F32)
    bb_re = f_re[..., None] * br - f_im[..., None] * bi
    bb_im = f_re[..., None] * bi + f_im[..., None] * br
    pw_re, pw_im = [jnp.ones_like(abar_re)], [jnp.zeros_like(abar_im)]
    for _ in range(L):
        pr, pi = pw_re[-1], pw_im[-1]
        pw_re.append(pr * abar_re - pi * abar_im)
        pw_im.append(pr * abar_im + pi * abar_re)
    pw_re, pw_im = jnp.stack(pw_re, 1), jnp.stack(pw_im, 1)
    cr, ci = c_re.astype(F32), c_im.astype(F32)
    cp_re = cr[:, None] * pw_re[:, :, None, :] - ci[:, None] * pw_im[:, :, None, :]
    cp_im = cr[:, None] * pw_im[:, :, None, :] + ci[:, None] * pw_re[:, :, None, :]
    kern = (jnp.einsum('gtop,gpi->gtoi', cp_re[:, :L], bb_re, precision=hp)
            - jnp.einsum('gtop,gpi->gtoi', cp_im[:, :L], bb_im, precision=hp))
    s_idx = jnp.arange(L)[:, None]
    t_idx = jnp.arange(L)[None, :]
    lag = t_idx - s_idx
    toep = jnp.where((lag >= 0)[None, :, :, None, None], kern[:, jnp.clip(lag, 0, L - 1)], 0.0)
    G = ar.shape[0]
    hc = SSM_GROUP
    mt = jnp.transpose(toep, (0, 1, 4, 2, 3)).reshape(G, L * hc, L * hc)
    rev_re, rev_im = pw_re[:, L - 1::-1][:, :L], pw_im[:, L - 1::-1][:, :L]
    w_re = (rev_re[:, :, None, :] * jnp.swapaxes(bb_re, 1, 2)[:, None]
            - rev_im[:, :, None, :] * jnp.swapaxes(bb_im, 1, 2)[:, None]).reshape(G, L * hc, SSM_STATE)
    w_im = (rev_re[:, :, None, :] * jnp.swapaxes(bb_im, 1, 2)[:, None]
            + rev_im[:, :, None, :] * jnp.swapaxes(bb_re, 1, 2)[:, None]).reshape(G, L * hc, SSM_STATE)
    v_re = jnp.transpose(cp_re[:, 1:], (0, 3, 1, 2)).reshape(G, SSM_STATE, L * hc)
    v_im = -jnp.transpose(cp_im[:, 1:], (0, 3, 1, 2)).reshape(G, SSM_STATE, L * hc)
    lam = jnp.stack([pw_re[:, L], pw_im[:, L]], axis=1)
    return mt, w_re, w_im, v_re, v_im, lam


def _s5_kernel(u_ref, mt_ref, wre_ref, wim_ref, vre_ref, vim_ref, lam_ref, y_ref,
               sre_s, sim_s, hre_s, him_s, *, batch):
    u = u_ref[...]
    sre_s[...] = _dot3(u, wre_ref[...])
    sim_s[...] = _dot3(u, wim_ref[...])
    lre = lam_ref[0:1, :]
    lim = lam_ref[1:2, :]
    n_chunks = u.shape[0] // batch
    state_w = lre.shape[1]

    def step(c, carry):
        out = []
        for b in range(batch):
            hr, hi = carry[b]
            row = b * n_chunks + c
            hre_s[pl.ds(row, 1), :] = hr
            him_s[pl.ds(row, 1), :] = hi
            sr = sre_s[pl.ds(row, 1), :]
            si = sim_s[pl.ds(row, 1), :]
            out.append((lre * hr - lim * hi + sr, lre * hi + lim * hr + si))
        return tuple(out)

    zero = jnp.zeros((1, state_w), F32)
    lax.fori_loop(0, n_chunks, step, tuple((zero, zero) for _ in range(batch)))
    y_ref[...] = (_dot3(u, mt_ref[...]) + _dot3(hre_s[...], vre_ref[...])
                  + _dot3(him_s[...], vim_ref[...]))


def _s5_scan(u_chunks, ops, batch):
    mt, w_re, w_im, v_re, v_im, lam = ops
    G, nc, cw = u_chunks.shape
    P = SSM_STATE
    grp = lambda *shape: pl.BlockSpec((None,) + shape, lambda g: (g,) + (0,) * len(shape))
    return pl.pallas_call(
        functools.partial(_s5_kernel, batch=batch),
        grid=(G,),
        in_specs=[grp(nc, cw), grp(cw, cw), grp(cw, P), grp(cw, P), grp(P, cw), grp(P, cw), grp(2, P)],
        out_specs=grp(nc, cw),
        out_shape=jax.ShapeDtypeStruct((G, nc, cw), F32),
        scratch_shapes=[pltpu.VMEM((nc, P), F32)] * 4,
        compiler_params=_params(1),
        name="s5_scan",
    )(u_chunks, mt, w_re, w_im, v_re, v_im, lam)


def _glu_kernel(y_ref, u_ref, d_ref, w_ref, b_ref, o_ref):
    y = y_ref[...] + d_ref[...] * u_ref[...]
    y = 0.5 * y * (1.0 + jnp.tanh(math.sqrt(2.0 / math.pi) * (y + 0.044715 * (y * y * y))))
    gate = jax.nn.sigmoid(_dot(y.astype(BF16), w_ref[...]) + b_ref[...])
    o_ref[...] = (y * gate).astype(o_ref.dtype)


def _s5_glu(y_scan, side, d_skip, w_glu, b_glu, tm):
    n, width = y_scan.shape
    row = lambda: pl.BlockSpec((1, width), lambda i: (0, 0))
    return pl.pallas_call(
        _glu_kernel,
        grid=(n // tm,),
        in_specs=[
            pl.BlockSpec((tm, width), lambda i: (i, 0)),
            pl.BlockSpec((tm, width), lambda i: (i, 0)),
            row(),
            pl.BlockSpec((width, width), lambda i: (0, 0)),
            row(),
        ],
        out_specs=pl.BlockSpec((tm, width), lambda i: (i, 0)),
        out_shape=jax.ShapeDtypeStruct((n, width), BF16),
        compiler_params=_params(1),
        name="s5_glu",
    )(y_scan, side, d_skip.reshape(1, width), w_glu.astype(BF16), b_glu.reshape(1, width))


def _outproj_kernel(sb_ref, ssm_ref, dsa_ref, w_ref, x_ref, gate_ref, g_ref, b_ref, sc_ref, sh_ref,
                    x_out_ref, u_out_ref, *, alpha):
    k0 = sb_ref.shape[1]
    k1 = k0 + ssm_ref.shape[1]
    mix = (_dot(sb_ref[...], w_ref[0:k0, :]) + _dot(ssm_ref[...], w_ref[k0:k1, :])
           + _dot(dsa_ref[...], w_ref[k1:, :]))
    y = alpha * x_ref[...] + (1.0 + gate_ref[...]) * mix
    x1 = _layer_norm(y, g_ref[...], b_ref[...])
    x_out_ref[...] = x1
    u_out_ref[...] = (x1 * (1.0 + sc_ref[...]) + sh_ref[...]).astype(u_out_ref.dtype)


def _outproj_ln(sb_o, ssm_o, dsa_o, w_out, x, gate, ln_g, ln_b, scale, shift, seq_len, alpha, tm):
    n, d = x.shape
    per_batch = lambda: pl.BlockSpec((None, 1, d), lambda i: ((i * tm) // seq_len, 0, 0))
    vec = lambda: pl.BlockSpec((1, d), lambda i: (0, 0))
    tile = lambda w: pl.BlockSpec((tm, w), lambda i: (i, 0))
    return pl.pallas_call(
        functools.partial(_outproj_kernel, alpha=alpha),
        grid=(n // tm,),
        in_specs=[tile(sb_o.shape[1]), tile(ssm_o.shape[1]), tile(dsa_o.shape[1]),
                  pl.BlockSpec(w_out.shape, lambda i: (0, 0)),
                  tile(d), per_batch(), vec(), vec(), per_batch(), per_batch()],
        out_specs=[tile(d), tile(d)],
        out_shape=[jax.ShapeDtypeStruct((n, d), F32), jax.ShapeDtypeStruct((n, d), BF16)],
        compiler_params=_params(1),
        name="outproj_ln",
    )(sb_o, ssm_o, dsa_o, w_out, x, gate, ln_g.reshape(1, d), ln_b.reshape(1, d), scale, shift)


def _ffn_up_kernel(u_ref, halo_ref, wg_ref, wv_ref, cwg_ref, cwv_ref, cbg_ref, cbv_ref, o_ref,
                   ag_s, av_s, *, tm, halo, seq_len):
    i = pl.program_id(0)
    has_prev = jnp.where((i * tm) % seq_len == 0, 0.0, 1.0)
    u = u_ref[...]
    uh = halo_ref[...]
    ag_s[0:halo, :] = _dot(uh, wg_ref[...]) * has_prev
    av_s[0:halo, :] = _dot(uh, wv_ref[...]) * has_prev
    ag_s[halo:, :] = _dot(u, wg_ref[...])
    av_s[halo:, :] = _dot(u, wv_ref[...])

    def conv(a_s, cw_ref, cb_ref):
        out = cb_ref[...]
        for tap in range(CONV_WIDTH):
            back = CONV_WIDTH - 1 - tap
            out = out + cw_ref[tap:tap + 1, :] * a_s[halo - back:halo - back + tm, :]
        return out

    g = conv(ag_s, cwg_ref, cbg_ref)
    val = conv(av_s, cwv_ref, cbv_ref)
    o_ref[...] = (g * jax.nn.sigmoid(g) * val).astype(o_ref.dtype)


def _ffn_up(u, w_up, conv_w, conv_b, seq_len, tm, tf):
    n, d = u.shape
    ffp = w_up.shape[1] // 2
    nj = ffp // tf
    halo = 16
    return pl.pallas_call(
        functools.partial(_ffn_up_kernel, tm=tm, halo=halo, seq_len=seq_len),
        grid=(n // tm, nj),
        in_specs=[
            pl.BlockSpec((tm, d), lambda i, j: (i, 0)),
            pl.BlockSpec((halo, d), lambda i, j: (jnp.maximum(i * (tm // halo) - 1, 0), 0)),
            pl.BlockSpec((d, tf), lambda i, j: (0, j)),
            pl.BlockSpec((d, tf), lambda i, j: (0, j + nj)),
            pl.BlockSpec((CONV_WIDTH, tf), lambda i, j: (0, j)),
            pl.BlockSpec((CONV_WIDTH, tf), lambda i, j: (0, j + nj)),
            pl.BlockSpec((1, tf), lambda i, j: (0, j)),
            pl.BlockSpec((1, tf), lambda i, j: (0, j + nj)),
        ],
        out_specs=pl.BlockSpec((tm, tf), lambda i, j: (i, j)),
        out_shape=jax.ShapeDtypeStruct((n, ffp), BF16),
        scratch_shapes=[pltpu.VMEM((tm + halo, tf), F32)] * 2,
        compiler_params=_params(2),
        name="ffn_up_conv_gate",
    )(u, u, w_up, w_up, conv_w, conv_w, conv_b, conv_b)


def _ffn_down_kernel(h_ref, w_ref, x_ref, gate_ref, g_ref, b_ref, o_ref, acc_s, *, alpha):
    k = pl.program_id(1)

    @pl.when(k == 0)
    def _():
        acc_s[...] = jnp.zeros_like(acc_s)

    acc_s[...] += _dot(h_ref[...], w_ref[...])

    @pl.when(k == pl.num_programs(1) - 1)
    def _():
        y = alpha * x_ref[...] + (1.0 + gate_ref[...]) * acc_s[...]
        o_ref[...] = _layer_norm(y, g_ref[...], b_ref[...])


def _ffn_down_ln(h, w_down, x, gate, ln_g, ln_b, seq_len, alpha, tm, tk):
    n, d = x.shape
    ffp = h.shape[1]
    return pl.pallas_call(
        functools.partial(_ffn_down_kernel, alpha=alpha),
        grid=(n // tm, ffp // tk),
        in_specs=[
            pl.BlockSpec((tm, tk), lambda i, k: (i, k)),
            pl.BlockSpec((tk, d), lambda i, k: (k, 0)),
            pl.BlockSpec((tm, d), lambda i, k: (i, 0)),
            pl.BlockSpec((None, 1, d), lambda i, k: ((i * tm) // seq_len, 0, 0)),
            pl.BlockSpec((1, d), lambda i, k: (0, 0)),
            pl.BlockSpec((1, d), lambda i, k: (0, 0)),
        ],
        out_specs=pl.BlockSpec((tm, d), lambda i, k: (i, 0)),
        out_shape=jax.ShapeDtypeStruct((n, d), F32),
        scratch_shapes=[pltpu.VMEM((tm, d), F32)],
        compiler_params=_params(2),
        name="ffn_down_ln",
    )(h, w_down, x, gate, ln_g.reshape(1, d), ln_b.reshape(1, d))


def _round_up(v, m):
    return (v + m - 1) // m * m


def _split_w_in(w_in_l):
    sizes = (SB_WIDTH, SB_WIDTH, SB_WIDTH, w_in_l.shape[1] - 3 * SB_WIDTH - 3 * DSA_WIDTH
             - IDX_HEADS * IDX_DIM - IDX_DIM - IDX_HEADS, DSA_WIDTH, DSA_WIDTH, DSA_WIDTH,
             IDX_HEADS * IDX_DIM, IDX_DIM, IDX_HEADS)
    offs = [0]
    for s in sizes:
        offs.append(offs[-1] + s)
    sbq, sbk, sbv, ssm, dq, dk, dv, iq, ik, iw = (w_in_l[:, offs[t]:offs[t + 1]] for t in range(10))
    d = w_in_l.shape[0]
    z = lambda w: jnp.zeros((d, w), w_in_l.dtype)
    main = jnp.concatenate([dq, dk, dv, sbq, sbk, sbv, ssm, iq], axis=1).astype(BF16)
    side = jnp.concatenate([ssm, ik, z(LANE - IDX_DIM), z(LANE - IDX_DIM), ik, iw, z(LANE - IDX_HEADS)],
                           axis=1).astype(BF16)
    return main, side, sizes[3]


def kernel(x, c, w_ada, b_ada, w_in, w_out, ssm_a_re, ssm_a_im, ssm_log_dt, ssm_b_re, ssm_b_im,
           ssm_c_re, ssm_c_im, ssm_d, w_glu, b_glu, rel_bias, ln1_g, ln1_b, w_up, conv_w, conv_b,
           w_down, ln2_g, ln2_b):
    batch, seq_len, d_model = x.shape
    depth = w_ada.shape[0]
    n = batch * seq_len
    alpha = (2 * depth) ** 0.25
    d_ff = w_down.shape[1]
    ffp = _round_up(d_ff, 512)
    tm_proj = min(1024, seq_len)
    tm = min(512, seq_len)

    mod = _adaln_mod(c, w_ada, b_ada)
    xf = x.reshape(n, d_model)
    for l in range(depth):
        sh_m, sc_m, g_m, sh_f, sc_f, g_f = (
            mod[l, :batch, t * d_model:(t + 1) * d_model].reshape(batch, 1, d_model) for t in range(N_MOD))
        w_main, w_side, ssm_width = _split_w_in(w_in[l])
        proj = _modulated_matmul(xf, sc_m, sh_m, w_main, BF16, seq_len, tm_proj, 512, "in_proj")
        side = _modulated_matmul(xf, sc_m, sh_m, w_side, F32, seq_len, tm_proj, w_side.shape[1], "in_proj_side")

        sb_o = _sb_attention(proj, batch, seq_len, 3 * DSA_HEADS, 3 * DSA_HEADS + SB_HEADS,
                             3 * DSA_HEADS + 2 * SB_HEADS)
        side_blk = ssm_width // LANE
        dsa_cols = {"dq": 0, "dk": 1, "dv": 2,
                    "iq": (3 * DSA_WIDTH + 3 * SB_WIDTH + ssm_width) // (IDX_HEADS * IDX_DIM),
                    "ik_even": side_blk, "ik_odd": side_blk + 1, "iw": side_blk + 2}
        dsa_o = _dsa_attention(proj, side, rel_bias, batch, seq_len, dsa_cols)

        groups = ssm_width // SSM_GROUP
        u_chunks = jnp.transpose(
            side[:, :ssm_width].reshape(n // SSM_CHUNK, SSM_CHUNK, groups, SSM_GROUP), (2, 0, 1, 3)
        ).reshape(groups, n // SSM_CHUNK, SSM_CHUNK * SSM_GROUP)
        ops = _s5_operators(ssm_a_re[l], ssm_a_im[l], ssm_log_dt[l], ssm_b_re[l], ssm_b_im[l],
                            ssm_c_re[l], ssm_c_im[l])
        y_chunks = _s5_scan(u_chunks, ops, batch)
        y_scan = jnp.transpose(
            y_chunks.reshape(groups, n // SSM_CHUNK, SSM_CHUNK, SSM_GROUP), (1, 2, 0, 3)).reshape(n, ssm_width)
        ssm_o = _s5_glu(y_scan, side, ssm_d[l], w_glu[l], b_glu[l], tm)

        x1, u_f = _outproj_ln(sb_o, ssm_o, dsa_o, w_out[l].astype(BF16), xf, g_m, ln1_g[l], ln1_b[l],
                              sc_f, sh_f, seq_len, alpha, tm)

        pad_cols = lambda a: jnp.pad(a, ((0, 0), (0, ffp - d_ff)))
        w_up_p = jnp.concatenate([pad_cols(w_up[l][:, :d_ff]), pad_cols(w_up[l][:, d_ff:])], axis=1).astype(BF16)
        conv_w_p = jnp.concatenate([pad_cols(conv_w[l][:, :d_ff]), pad_cols(conv_w[l][:, d_ff:])], axis=1)
        conv_b_p = jnp.concatenate([pad_cols(conv_b[l][None, :d_ff]), pad_cols(conv_b[l][None, d_ff:])], axis=1)
        w_down_p = jnp.pad(w_down[l], ((0, ffp - d_ff), (0, 0))).astype(BF16)
        h = _ffn_up(u_f, w_up_p, conv_w_p, conv_b_p, seq_len, tm, 512)
        xf = _ffn_down_ln(h, w_down_p, x1, g_f, ln2_g[l], ln2_b[l], seq_len, alpha, tm, 512)
    return xf.reshape(batch, seq_len, d_model)
```

```python
import functools
import math

import jax
import jax.numpy as jnp
from jax import lax
from jax.experimental import pallas as pl
from jax.experimental.pallas import tpu as pltpu

F32 = jnp.float32
BF16 = jnp.bfloat16
I32 = jnp.int32

CHUNK = 64
CHUNK_SHIFT = CHUNK.bit_length() - 1
assert 1 << CHUNK_SHIFT == CHUNK
SB_HEADS = 6
DSA_HEADS = 6
HEAD_DIM = 128
SB_WIDTH = SB_HEADS * HEAD_DIM
DSA_WIDTH = DSA_HEADS * HEAD_DIM
SSM_GROUP = 16
SSM_STATE = 64
IDX_HEADS = 16
IDX_DIM = 64
DSA_MAX_TOPK = 256
REL_BUCKETS = 32
REL_MAX_DIST = 128
CONV_WIDTH = 3
N_MOD = 6
LN_EPS = 1e-5

LANE = 128
VMEM_LIMIT_BYTES = 56 * 1024 * 1024

SSM_CHUNK = 16
SB_BLOCK = 256
SB_ZERO_LOG = 104.0
DSA_TQ = 128
DSA_TK = 256
DSA_SUPER = 4
KEY_NEG_INF = -2139095041
INT_MIN = -2147483648
MASKED_LOGIT = -1e30


def _params(n_axes):
    return pltpu.CompilerParams(dimension_semantics=("arbitrary",) * n_axes,
                                vmem_limit_bytes=VMEM_LIMIT_BYTES)


def _dot(a, b):
    return jnp.dot(a, b, preferred_element_type=F32)


def _dot_nt(a, b):
    return lax.dot_general(a, b, (((1,), (1,)), ((), ())), preferred_element_type=F32)


def _split_bf16(a):
    hi = a.astype(BF16)
    lo = (a - hi.astype(F32)).astype(BF16)
    return hi, lo


def _dot3(a, b):
    ah, al = _split_bf16(a)
    bh, bl = _split_bf16(b)
    return _dot(ah, bh) + _dot(ah, bl) + _dot(al, bh)


def _layer_norm(y, g, b):
    mu = jnp.mean(y, axis=-1, keepdims=True)
    d = y - mu
    var = jnp.mean(d * d, axis=-1, keepdims=True)
    return d * lax.rsqrt(var + LN_EPS) * g + b


def _mod_kernel(c_ref, w_ref, b_ref, o_ref):
    c = c_ref[...]
    cond = c * jax.nn.sigmoid(c)
    o_ref[...] = _dot(cond.astype(BF16), w_ref[...].astype(BF16)) + b_ref[...]


def _adaln_mod(c, w_ada, b_ada):
    depth, d_model, n_out = w_ada.shape
    rows = 8
    c_pad = jnp.zeros((rows, d_model), F32).at[: c.shape[0]].set(c)
    tn = 1024
    return pl.pallas_call(
        _mod_kernel,
        grid=(depth, n_out // tn),
        in_specs=[
            pl.BlockSpec((rows, d_model), lambda l, j: (0, 0)),
            pl.BlockSpec((None, d_model, tn), lambda l, j: (l, 0, j)),
            pl.BlockSpec((None, 1, tn), lambda l, j: (l, 0, j)),
        ],
        out_specs=pl.BlockSpec((None, rows, tn), lambda l, j: (l, 0, j)),
        out_shape=jax.ShapeDtypeStruct((depth, rows, n_out), F32),
        compiler_params=_params(2),
        name="adaln_mod",
    )(c_pad, w_ada, b_ada.reshape(depth, 1, n_out))


def _inproj_kernel(x_ref, sc_ref, sh_ref, w_ref, o_ref, u_ref):
    @pl.when(pl.program_id(1) == 0)
    def _():
        u_ref[...] = (x_ref[...] * (1.0 + sc_ref[...]) + sh_ref[...]).astype(BF16)

    o_ref[...] = _dot(u_ref[...], w_ref[...]).astype(o_ref.dtype)


def _modulated_matmul(x, scale, shift, w, out_dtype, seq_len, tm, tn, name):
    n, d = x.shape
    ncols = w.shape[1]
    return pl.pallas_call(
        _inproj_kernel,
        grid=(n // tm, ncols // tn),
        in_specs=[
            pl.BlockSpec((tm, d), lambda i, j: (i, 0)),
            pl.BlockSpec((None, 1, d), lambda i, j: ((i * tm) // seq_len, 0, 0)),
            pl.BlockSpec((None, 1, d), lambda i, j: ((i * tm) // seq_len, 0, 0)),
            pl.BlockSpec((d, tn), lambda i, j: (0, j)),
        ],
        out_specs=pl.BlockSpec((tm, tn), lambda i, j: (i, j)),
        out_shape=jax.ShapeDtypeStruct((n, ncols), out_dtype),
        scratch_shapes=[pltpu.VMEM((tm, d), BF16)],
        compiler_params=_params(2),
        name=name,
    )(x, scale, shift, w)


def _sb_kernel(q_ref, k_ref, v_ref, o_ref, acc_ref, run_ref, *, blk, scale):
    i = pl.program_id(2)
    q = q_ref[...]
    rows = lax.broadcasted_iota(I32, (blk, blk), 0)
    cols = lax.broadcasted_iota(I32, (blk, blk), 1)
    tri = jnp.where(rows > cols, 1.0, 0.0).astype(BF16)
    keep = cols < rows

    def block(kb, masked):
        start = pl.multiple_of(kb * blk, blk)
        k = k_ref[pl.ds(start, blk), :]
        v = v_ref[pl.ds(start, blk), :]
        z = _dot_nt(q, k) * scale
        softplus = jnp.maximum(z, 0.0) + jnp.log(1.0 + jnp.exp(-jnp.abs(z)))
        log_1m = -softplus
        if masked:
            log_1m = jnp.where(keep, log_1m, 0.0)
        hi, lo = _split_bf16(log_1m)
        suffix = _dot(hi, tri) + _dot(lo, tri)
        run = run_ref[...]
        w = jnp.exp((z - softplus) + suffix + run)
        if masked:
            w = jnp.where(keep, w, 0.0)
        acc_ref[...] += _dot(w.astype(BF16), v)
        run_new = run + jnp.sum(log_1m, axis=1, keepdims=True)
        run_ref[...] = run_new
        return jnp.max(run_new)

    acc_ref[...] = jnp.zeros_like(acc_ref)
    run_ref[...] = jnp.zeros_like(run_ref)
    top = block(i, True)

    def cond(carry):
        kb, top = carry
        return jnp.logical_and(kb >= 0, top > -SB_ZERO_LOG)

    def body(carry):
        kb, _ = carry
        return kb - 1, block(kb, False)

    lax.while_loop(cond, body, (i - 1, top))
    o_ref[...] = acc_ref[...].astype(o_ref.dtype)


def _sb_attention(proj, batch, seq_len, q_col, k_col, v_col):
    n = proj.shape[0]
    blk = min(SB_BLOCK, seq_len)
    nq = seq_len // blk
    kern = functools.partial(_sb_kernel, blk=blk, scale=HEAD_DIM ** -0.5)
    return pl.pallas_call(
        kern,
        grid=(batch, SB_HEADS, nq),
        in_specs=[
            pl.BlockSpec((blk, HEAD_DIM), lambda b, h, i: (b * nq + i, q_col + h)),
            pl.BlockSpec((seq_len, HEAD_DIM), lambda b, h, i: (b, k_col + h)),
            pl.BlockSpec((seq_len, HEAD_DIM), lambda b, h, i: (b, v_col + h)),
        ],
        out_specs=pl.BlockSpec((blk, HEAD_DIM), lambda b, h, i: (b * nq + i, h)),
        out_shape=jax.ShapeDtypeStruct((n, SB_WIDTH), BF16),
        scratch_shapes=[pltpu.VMEM((blk, HEAD_DIM), F32), pltpu.VMEM((blk, 1), F32)],
        compiler_params=_params(3),
        name="sb_attention",
    )(proj, proj, proj)


def _t5_bucket(rel):
    half = REL_BUCKETS // 2
    max_exact = half // 2
    n = jnp.abs(rel)
    nf = jnp.maximum(n, 1).astype(F32)
    large = max_exact + (jnp.log(nf / max_exact) / math.log(REL_MAX_DIST / max_exact)
                         * (half - max_exact)).astype(I32)
    large = jnp.minimum(large, half - 1)
    return jnp.where(rel > 0, half, 0) + jnp.where(n < max_exact, n, large)


def _dsa_kernel(rb_ref, qd_ref, kd_ref, vd_ref, iq_ref, ika_ref, ikb_ref, iw_ref, o_ref,
                ka_s, kb_s, wb_s, keys_s, bias_s, acc_s, m_s, l_s, j_s, prefix_s, cnt_s, mask_s, logit_s,
                *, tq, tk, sup, topk, seq_len, scale):
    i = pl.program_id(1)
    q0 = i * tq
    n_kb = (q0 + tq + tk - 1) // tk
    bias_w = bias_s.shape[2]
    bias_shift = bias_w - tk

    @pl.when(i == 0)
    def _setup():
        ka_s[...] = ika_ref[...].astype(BF16)
        kb_s[...] = ikb_ref[...].astype(BF16)
        r = lax.broadcasted_iota(I32, (tq, bias_w), 0)
        c = lax.broadcasted_iota(I32, (tq, bias_w), 1)
        bucket = _t5_bucket(c - r - bias_shift)
        for h in range(DSA_HEADS):
            bias = jnp.zeros((tq, bias_w), F32)
            for j in range(REL_BUCKETS):
                bias = jnp.where(bucket == j, rb_ref[j, h], bias)
            bias_s[h] = bias * (1.0 / math.log(2.0))

    iw = iw_ref[...]
    for h in range(IDX_HEADS):
        wb_s[h] = jnp.broadcast_to(iw[:, h:h + 1] * (IDX_DIM ** -0.5 * IDX_HEADS ** -0.5), (tq, LANE))

    rows = lax.broadcasted_iota(I32, (tq, tk), 0)
    cols = lax.broadcasted_iota(I32, (tq, tk), 1)
    q_chunk = (q0 + rows) >> CHUNK_SHIFT

    def score_block(kb, carry):
        start = pl.multiple_of(kb * tk, tk)
        k_even = ka_s[pl.ds(start, tk), :]
        k_odd = kb_s[pl.ds(start, tk), :]
        halves = [jnp.zeros((tq, LANE), F32) for _ in range(tk // LANE)]
        for pair in range(IDX_HEADS // 2):
            lhs = iq_ref[:, pair * LANE:(pair + 1) * LANE]
            for kk, h in ((k_even, 2 * pair), (k_odd, 2 * pair + 1)):
                d = _dot_nt(lhs, kk)
                w = wb_s[h]
                for s in range(tk // LANE):
                    halves[s] = halves[s] + jnp.maximum(d[:, s * LANE:(s + 1) * LANE], 0.0) * w
        score = jnp.concatenate(halves, axis=1)
        score = jnp.where(score == 0.0, 0.0, score)
        admissible = ((start + cols) >> CHUNK_SHIFT) <= q_chunk
        score = jnp.where(admissible, score, -jnp.inf)
        bits = pltpu.bitcast(score, I32)
        keys_s[:, pl.ds(start, tk)] = jnp.where(bits < 0, bits ^ 0x7FFFFFFF, bits)
        return carry

    lax.fori_loop(0, n_kb, score_block, 0)

    def pad_block(kb, carry):
        keys_s[:, pl.ds(pl.multiple_of(kb * tk, tk), tk)] = jnp.full((tq, tk), KEY_NEG_INF, I32)
        return carry

    lax.fori_loop(n_kb, (n_kb + sup - 1) // sup * sup, pad_block, 0)

    lane_pos = lax.broadcasted_iota(I32, (tq, LANE), 1)

    def count(pred):
        def body(kb, acc):
            for s in range(tk // LANE):
                first = pl.multiple_of(kb * tk + s * LANE, LANE)
                acc = acc + jnp.where(pred(keys_s[:, pl.ds(first, LANE)], first), 1.0, 0.0)
            return acc
        acc = lax.fori_loop(0, n_kb, body, jnp.zeros((tq, LANE), F32))
        return jnp.broadcast_to(jnp.sum(acc, axis=1, keepdims=True), (tq, LANE))

    prefix_s[...] = jnp.full((tq, LANE), INT_MIN, I32)
    cnt_s[...] = jnp.full((tq, LANE), tk, F32) * n_kb.astype(F32)

    def select_cond(carry):
        it, settled = carry
        return jnp.logical_and(it < 32, settled == 0)

    def select_bit(carry):
        it, _ = carry
        bit = 31 - it
        prefix = prefix_s[...]
        cand = jnp.where(bit == 31, jnp.zeros_like(prefix), prefix | jnp.left_shift(1, bit))
        cnt = count(lambda k, first: k >= cand)
        take = cnt >= topk
        prefix_s[...] = jnp.where(take, cand, prefix)
        cnt_new = jnp.where(take, cnt, cnt_s[...])
        cnt_s[...] = cnt_new
        settled = jnp.max(jnp.abs(cnt_new - topk)) == 0.0
        return it + 1, settled.astype(I32)

    lax.while_loop(select_cond, select_bit, (jnp.int32(0), jnp.int32(0)))
    kth = prefix_s[...]

    tied = jnp.where(kth > KEY_NEG_INF, cnt_s[...] - topk, 0.0)
    j_s[...] = jnp.full((tq, LANE), seq_len, I32)

    @pl.when(jnp.max(tied) > 0.0)
    def _break_ties():
        need = topk - count(lambda k, first: k > kth)

        def pos_bit(it, last):
            bit = (seq_len - 1).bit_length() - 1 - it
            cand = last | jnp.left_shift(1, bit)
            cnt = count(lambda k, first: jnp.where((first + lane_pos) < cand, k, INT_MIN) == kth)
            return jnp.where(cnt < need, cand, last)
        j_s[...] = lax.fori_loop(0, (seq_len - 1).bit_length(), pos_bit, jnp.zeros((tq, LANE), I32))

    last_tied = j_s[...]
    kth_eff = jnp.maximum(kth, KEY_NEG_INF + 1)

    log2e = 1.0 / math.log(2.0)
    m_s[...] = jnp.full_like(m_s, MASKED_LOGIT)
    l_s[...] = jnp.zeros_like(l_s)
    acc_s[...] = jnp.zeros_like(acc_s)
    ones_col = jnp.ones((tk, LANE), BF16)
    n_sup = (n_kb + sup - 1) // sup

    def attend_super(sb, carry):
        base = pl.multiple_of(sb * (sup * tk), sup * tk)
        for t in range(sup * tk // LANE):
            first = pl.multiple_of(base + t * LANE, LANE)
            sel = keys_s[:, pl.ds(first, LANE)] >= jnp.where((first + lane_pos) > last_tied, kth_eff + 1, kth_eff)
            mask_s[:, t * LANE:(t + 1) * LANE] = jnp.where(sel, 0.0, -jnp.inf)
        for h in range(DSA_HEADS):
            hs = slice(h * HEAD_DIM, (h + 1) * HEAD_DIM)
            qh = qd_ref[:, hs]
            top = jnp.full((tq, LANE), -jnp.inf, F32)
            for j in range(sup):
                start = base + j * tk
                off = pl.multiple_of(jnp.clip(start - q0 + bias_shift, 0, bias_shift), LANE)
                s = (_dot_nt(qh, kd_ref[pl.ds(start, tk), hs]) * (scale * log2e)
                     + bias_s[h, :, pl.ds(off, tk)] + mask_s[:, j * tk:(j + 1) * tk])
                logit_s[h, :, j * tk:(j + 1) * tk] = s
                for t in range(tk // LANE):
                    top = jnp.maximum(top, s[:, t * LANE:(t + 1) * LANE])
            m_old = m_s[h]
            m_new = jnp.maximum(m_old, jnp.max(top, axis=1, keepdims=True))
            alpha = jnp.exp2(m_old - m_new)
            m_s[h] = m_new
            pv = jnp.zeros((tq, HEAD_DIM), F32)
            psum = jnp.zeros((tq, LANE), F32)
            for j in range(sup):
                start = base + j * tk
                p = jnp.concatenate(
                    [jnp.exp2(logit_s[h, :, j * tk + t * LANE:j * tk + (t + 1) * LANE] - m_new).astype(BF16)
                     for t in range(tk // LANE)], axis=1)
                pv = pv + _dot(p, vd_ref[pl.ds(start, tk), hs])
                psum = psum + _dot(p, ones_col)
            acc_s[h] = alpha * acc_s[h] + pv
            l_s[h] = alpha * l_s[h] + psum
        return carry

    lax.fori_loop(0, n_sup, attend_super, 0)
    for h in range(DSA_HEADS):
        o_ref[:, h * HEAD_DIM:(h + 1) * HEAD_DIM] = (acc_s[h] / l_s[h]).astype(o_ref.dtype)


def _dsa_attention(proj, side, rel_bias, batch, seq_len, cols):
    n = proj.shape[0]
    tq = min(DSA_TQ, seq_len)
    tk = min(DSA_TK, seq_len)
    nq = seq_len // tq
    topk = min(DSA_MAX_TOPK, seq_len // 4)
    bias_w = tk + 2 * tk
    sup = min(DSA_SUPER, seq_len // tk)
    kern = functools.partial(_dsa_kernel, tq=tq, tk=tk, sup=sup, topk=topk, seq_len=seq_len,
                             scale=HEAD_DIM ** -0.5)
    once = pl.Buffered(1)
    return pl.pallas_call(
        kern,
        grid=(batch, nq),
        in_specs=[
            pl.BlockSpec(memory_space=pltpu.SMEM),
            pl.BlockSpec((tq, DSA_WIDTH), lambda b, i: (b * nq + i, cols["dq"])),
            pl.BlockSpec((seq_len, DSA_WIDTH), lambda b, i: (b, cols["dk"]), pipeline_mode=once),
            pl.BlockSpec((seq_len, DSA_WIDTH), lambda b, i: (b, cols["dv"]), pipeline_mode=once),
            pl.BlockSpec((tq, IDX_HEADS * IDX_DIM), lambda b, i: (b * nq + i, cols["iq"])),
            pl.BlockSpec((seq_len, LANE), lambda b, i: (b, cols["ik_even"]), pipeline_mode=once),
            pl.BlockSpec((seq_len, LANE), lambda b, i: (b, cols["ik_odd"]), pipeline_mode=once),
            pl.BlockSpec((tq, LANE), lambda b, i: (b * nq + i, cols["iw"])),
        ],
        out_specs=pl.BlockSpec((tq, DSA_WIDTH), lambda b, i: (b * nq + i, 0)),
        out_shape=jax.ShapeDtypeStruct((n, DSA_WIDTH), BF16),
        scratch_shapes=[
            pltpu.VMEM((seq_len, LANE), BF16),
            pltpu.VMEM((seq_len, LANE), BF16),
            pltpu.VMEM((IDX_HEADS, tq, LANE), F32),
            pltpu.VMEM((tq, seq_len), I32),
            pltpu.VMEM((DSA_HEADS, tq, bias_w), F32),
            pltpu.VMEM((DSA_HEADS, tq, HEAD_DIM), F32),
            pltpu.VMEM((DSA_HEADS, tq, LANE), F32),
            pltpu.VMEM((DSA_HEADS, tq, LANE), F32),
            pltpu.VMEM((tq, LANE), I32),
            pltpu.VMEM((tq, LANE), I32),
            pltpu.VMEM((tq, LANE), F32),
            pltpu.VMEM((tq, sup * tk), F32),
            pltpu.VMEM((DSA_HEADS, tq, sup * tk), F32),
        ],
        compiler_params=_params(2),
        name="dsa_attention",
    )(rel_bias, proj, proj, proj, proj, side, side, side)


def _s5_operators(a_re, a_im, log_dt, b_re, b_im, c_re, c_im):
    hp = lax.Precision.HIGHEST
    L = SSM_CHUNK
    ar, ai = a_re.astype(F32), a_im.astype(F32)
    dt = jnp.exp(log_dt.astype(F32))[:, None]
    mag = jnp.exp(dt * ar)
    abar_re, abar_im = mag * jnp.cos(dt * ai), mag * jnp.sin(dt * ai)
    den = ar * ar + ai * ai
    nr = abar_re - 1.0
    f_re = (nr * ar + abar_im * ai) / den
    f_im = (abar_im * ar - nr * ai) / den
    br, bi = b_re.astype(F32), b_im.astype(F32)
    bb_re = f_re[..., None] * br - f_im[..., None] * bi
    bb_im = f_re[..., None] * bi + f_im[..., None] * br
    pw_re, pw_im = [jnp.ones_like(abar_re)], [jnp.zeros_like(abar_im)]
    for _ in range(L):
        pr, pi = pw_re[-1], pw_im[-1]
        pw_re.append(pr * abar_re - pi * abar_im)
        pw_im.append(pr * abar_im + pi * abar_re)
    pw_re, pw_im = jnp.stack(pw_re, 1), jnp.stack(pw_im, 1)
    cr, ci = c_re.astype(F32), c_im.astype(F32)
    cp_re = cr[:, None] * pw_re[:, :, None, :] - ci[:, None] * pw_im[:, :, None, :]
    cp_im = cr[:, None] * pw_im[:, :, None, :] + ci[:, None] * pw_re[:, :, None, :]
    kern = (jnp.einsum('gtop,gpi->gtoi', cp_re[:, :L], bb_re, precision=hp)
            - jnp.einsum('gtop,gpi->gtoi', cp_im[:, :L], bb_im, precision=hp))
    s_idx = jnp.arange(L)[:, None]
    t_idx = jnp.arange(L)[None, :]
    lag = t_idx - s_idx
    toep = jnp.where((lag >= 0)[None, :, :, None, None], kern[:, jnp.clip(lag, 0, L - 1)], 0.0)
    G = ar.shape[0]
    hc = SSM_GROUP
    mt = jnp.transpose(toep, (0, 1, 4, 2, 3)).reshape(G, L * hc, L * hc)
    rev_re, rev_im = pw_re[:, L - 1::-1][:, :L], pw_im[:, L - 1::-1][:, :L]
    w_re = (rev_re[:, :, None, :] * jnp.swapaxes(bb_re, 1, 2)[:, None]
            - rev_im[:, :, None, :] * jnp.swapaxes(bb_im, 1, 2)[:, None]).reshape(G, L * hc, SSM_STATE)
    w_im = (rev_re[:, :, None, :] * jnp.swapaxes(bb_im, 1, 2)[:, None]
            + rev_im[:, :, None, :] * jnp.swapaxes(bb_re, 1, 2)[:, None]).reshape(G, L * hc, SSM_STATE)
    v_re = jnp.transpose(cp_re[:, 1:], (0, 3, 1, 2)).reshape(G, SSM_STATE, L * hc)
    v_im = -jnp.transpose(cp_im[:, 1:], (0, 3, 1, 2)).reshape(G, SSM_STATE, L * hc)
    lam = jnp.stack([pw_re[:, L], pw_im[:, L]], axis=1)
    return mt, w_re, w_im, v_re, v_im, lam


def _s5_kernel(u_ref, mt_ref, wre_ref, wim_ref, vre_ref, vim_ref, lam_ref, y_ref,
               sre_s, sim_s, hre_s, him_s, *, batch):
    u = u_ref[...]
    sre_s[...] = _dot3(u, wre_ref[...])
    sim_s[...] = _dot3(u, wim_ref[...])
    lre = lam_ref[0:1, :]
    lim = lam_ref[1:2, :]
    n_chunks = u.shape[0] // batch
    state_w = lre.shape[1]

    def step(c, carry):
        out = []
        for b in range(batch):
            hr, hi = carry[b]
            row = b * n_chunks + c
            hre_s[pl.ds(row, 1), :] = hr
            him_s[pl.ds(row, 1), :] = hi
            sr = sre_s[pl.ds(row, 1), :]
            si = sim_s[pl.ds(row, 1), :]
            out.append((lre * hr - lim * hi + sr, lre * hi + lim * hr + si))
        return tuple(out)

    zero = jnp.zeros((1, state_w), F32)
    lax.fori_loop(0, n_chunks, step, tuple((zero, zero) for _ in range(batch)))
    y_ref[...] = (_dot3(u, mt_ref[...]) + _dot3(hre_s[...], vre_ref[...])
                  + _dot3(him_s[...], vim_ref[...]))


def _s5_scan(u_chunks, ops, batch):
    mt, w_re, w_im, v_re, v_im, lam = ops
    G, nc, cw = u_chunks.shape
    P = SSM_STATE
    grp = lambda *shape: pl.BlockSpec((None,) + shape, lambda g: (g,) + (0,) * len(shape))
    return pl.pallas_call(
        functools.partial(_s5_kernel, batch=batch),
        grid=(G,),
        in_specs=[grp(nc, cw), grp(cw, cw), grp(cw, P), grp(cw, P), grp(P, cw), grp(P, cw), grp(2, P)],
        out_specs=grp(nc, cw),
        out_shape=jax.ShapeDtypeStruct((G, nc, cw), F32),
        scratch_shapes=[pltpu.VMEM((nc, P), F32)] * 4,
        compiler_params=_params(1),
        name="s5_scan",
    )(u_chunks, mt, w_re, w_im, v_re, v_im, lam)


def _glu_kernel(y_ref, u_ref, d_ref, w_ref, b_ref, o_ref):
    y = y_ref[...] + d_ref[...] * u_ref[...]
    y = 0.5 * y * (1.0 + jnp.tanh(math.sqrt(2.0 / math.pi) * (y + 0.044715 * (y * y * y))))
    gate = jax.nn.sigmoid(_dot(y.astype(BF16), w_ref[...]) + b_ref[...])
    o_ref[...] = (y * gate).astype(o_ref.dtype)


def _s5_glu(y_scan, side, d_skip, w_glu, b_glu, tm):
    n, width = y_scan.shape
    row = lambda: pl.BlockSpec((1, width), lambda i: (0, 0))
    return pl.pallas_call(
        _glu_kernel,
        grid=(n // tm,),
        in_specs=[
            pl.BlockSpec((tm, width), lambda i: (i, 0)),
            pl.BlockSpec((tm, width), lambda i: (i, 0)),
            row(),
            pl.BlockSpec((width, width), lambda i: (0, 0)),
            row(),
        ],
        out_specs=pl.BlockSpec((tm, width), lambda i: (i, 0)),
        out_shape=jax.ShapeDtypeStruct((n, width), BF16),
        compiler_params=_params(1),
        name="s5_glu",
    )(y_scan, side, d_skip.reshape(1, width), w_glu.astype(BF16), b_glu.reshape(1, width))


def _outproj_kernel(sb_ref, ssm_ref, dsa_ref, w_ref, x_ref, gate_ref, g_ref, b_ref, sc_ref, sh_ref,
                    x_out_ref, u_out_ref, *, alpha):
    k0 = sb_ref.shape[1]
    k1 = k0 + ssm_ref.shape[1]
    mix = (_dot(sb_ref[...], w_ref[0:k0, :]) + _dot(ssm_ref[...], w_ref[k0:k1, :])
           + _dot(dsa_ref[...], w_ref[k1:, :]))
    y = alpha * x_ref[...] + (1.0 + gate_ref[...]) * mix
    x1 = _layer_norm(y, g_ref[...], b_ref[...])
    x_out_ref[...] = x1
    u_out_ref[...] = (x1 * (1.0 + sc_ref[...]) + sh_ref[...]).astype(u_out_ref.dtype)


def _outproj_ln(sb_o, ssm_o, dsa_o, w_out, x, gate, ln_g, ln_b, scale, shift, seq_len, alpha, tm):
    n, d = x.shape
    per_batch = lambda: pl.BlockSpec((None, 1, d), lambda i: ((i * tm) // seq_len, 0, 0))
    vec = lambda: pl.BlockSpec((1, d), lambda i: (0, 0))
    tile = lambda w: pl.BlockSpec((tm, w), lambda i: (i, 0))
    return pl.pallas_call(
        functools.partial(_outproj_kernel, alpha=alpha),
        grid=(n // tm,),
        in_specs=[tile(sb_o.shape[1]), tile(ssm_o.shape[1]), tile(dsa_o.shape[1]),
                  pl.BlockSpec(w_out.shape, lambda i: (0, 0)),
                  tile(d), per_batch(), vec(), vec(), per_batch(), per_batch()],
        out_specs=[tile(d), tile(d)],
        out_shape=[jax.ShapeDtypeStruct((n, d), F32), jax.ShapeDtypeStruct((n, d), BF16)],
        compiler_params=_params(1),
        name="outproj_ln",
    )(sb_o, ssm_o, dsa_o, w_out, x, gate, ln_g.reshape(1, d), ln_b.reshape(1, d), scale, shift)


def _ffn_up_kernel(u_ref, halo_ref, wg_ref, wv_ref, cwg_ref, cwv_ref, cbg_ref, cbv_ref, o_ref,
                   *scratch, tm, halo, seq_len, sub):
    i = pl.program_id(0)
    has_prev = jnp.where((i * tm) % seq_len == 0, 0.0, 1.0)
    u = u_ref[...]
    uh = halo_ref[...]

    def conv(a_s, w_ref, cw_ref, cb_ref, cs):
        a_s[0:halo, :] = _dot(uh, w_ref[:, cs]) * has_prev
        a_s[halo:, :] = _dot(u, w_ref[:, cs])
        out = cb_ref[:, cs]
        for tap in range(CONV_WIDTH):
            back = CONV_WIDTH - 1 - tap
            out = out + cw_ref[tap:tap + 1, cs] * a_s[halo - back:halo - back + tm, :]
        return out

    for c in range(len(scratch) // 2):
        cs = slice(c * sub, c * sub + scratch[2 * c].shape[1])
        g = conv(scratch[2 * c], wg_ref, cwg_ref, cbg_ref, cs)
        val = conv(scratch[2 * c + 1], wv_ref, cwv_ref, cbv_ref, cs)
        o_ref[:, cs] = (g * jax.nn.sigmoid(g) * val).astype(o_ref.dtype)


def _ffn_up(u, w_up, conv_w, conv_b, seq_len, tm, tf):
    n, d = u.shape
    ffp = w_up.shape[1] // 2
    nj = ffp // tf
    halo = 16
    sub = 2 * LANE
    return pl.pallas_call(
        functools.partial(_ffn_up_kernel, tm=tm, halo=halo, seq_len=seq_len, sub=sub),
        grid=(n // tm, nj),
        in_specs=[
            pl.BlockSpec((tm, d), lambda i, j: (i, 0)),
            pl.BlockSpec((halo, d), lambda i, j: (jnp.maximum(i * (tm // halo) - 1, 0), 0)),
            pl.BlockSpec((d, tf), lambda i, j: (0, j)),
            pl.BlockSpec((d, tf), lambda i, j: (0, j + nj)),
            pl.BlockSpec((CONV_WIDTH, tf), lambda i, j: (0, j)),
            pl.BlockSpec((CONV_WIDTH, tf), lambda i, j: (0, j + nj)),
            pl.BlockSpec((1, tf), lambda i, j: (0, j)),
            pl.BlockSpec((1, tf), lambda i, j: (0, j + nj)),
        ],
        out_specs=pl.BlockSpec((tm, tf), lambda i, j: (i, j)),
        out_shape=jax.ShapeDtypeStruct((n, ffp), BF16),
        scratch_shapes=[pltpu.VMEM((tm + halo, min(sub, tf - c * sub)), F32)
                        for c in range(pl.cdiv(tf, sub)) for _ in range(2)],
        compiler_params=_params(2),
        name="ffn_up_conv_gate",
    )(u, u, w_up, w_up, conv_w, conv_w, conv_b, conv_b)


def _ffn_down_kernel(h_ref, w_ref, x_ref, gate_ref, g_ref, b_ref, o_ref, *, alpha):
    y = alpha * x_ref[...] + (1.0 + gate_ref[...]) * _dot(h_ref[...], w_ref[...])
    o_ref[...] = _layer_norm(y, g_ref[...], b_ref[...])


def _ffn_down_ln(h, w_down, x, gate, ln_g, ln_b, seq_len, alpha, tm):
    n, d = x.shape
    ffp = h.shape[1]
    return pl.pallas_call(
        functools.partial(_ffn_down_kernel, alpha=alpha),
        grid=(n // tm,),
        in_specs=[
            pl.BlockSpec((tm, ffp), lambda i: (i, 0)),
            pl.BlockSpec((ffp, d), lambda i: (0, 0), pipeline_mode=pl.Buffered(1)),
            pl.BlockSpec((tm, d), lambda i: (i, 0)),
            pl.BlockSpec((None, 1, d), lambda i: ((i * tm) // seq_len, 0, 0)),
            pl.BlockSpec((1, d), lambda i: (0, 0)),
            pl.BlockSpec((1, d), lambda i: (0, 0)),
        ],
        out_specs=pl.BlockSpec((tm, d), lambda i: (i, 0)),
        out_shape=jax.ShapeDtypeStruct((n, d), F32),
        compiler_params=_params(1),
        name="ffn_down_ln",
    )(h, w_down, x, gate, ln_g.reshape(1, d), ln_b.reshape(1, d))


def _round_up(v, m):
    return (v + m - 1) // m * m


def _split_w_in(w_in_l):
    sizes = (SB_WIDTH, SB_WIDTH, SB_WIDTH, w_in_l.shape[1] - 3 * SB_WIDTH - 3 * DSA_WIDTH
             - IDX_HEADS * IDX_DIM - IDX_DIM - IDX_HEADS, DSA_WIDTH, DSA_WIDTH, DSA_WIDTH,
             IDX_HEADS * IDX_DIM, IDX_DIM, IDX_HEADS)
    offs = [0]
    for s in sizes:
        offs.append(offs[-1] + s)
    sbq, sbk, sbv, ssm, dq, dk, dv, iq, ik, iw = (w_in_l[:, offs[t]:offs[t + 1]] for t in range(10))
    d = w_in_l.shape[0]
    z = lambda w: jnp.zeros((d, w), w_in_l.dtype)
    main = jnp.concatenate([dq, dk, dv, sbq, sbk, sbv, ssm, iq], axis=1).astype(BF16)
    side = jnp.concatenate([ssm, ik, z(LANE - IDX_DIM), z(LANE - IDX_DIM), ik, iw, z(LANE - IDX_HEADS)],
                           axis=1).astype(BF16)
    return main, side, sizes[3]


def kernel(x, c, w_ada, b_ada, w_in, w_out, ssm_a_re, ssm_a_im, ssm_log_dt, ssm_b_re, ssm_b_im,
           ssm_c_re, ssm_c_im, ssm_d, w_glu, b_glu, rel_bias, ln1_g, ln1_b, w_up, conv_w, conv_b,
           w_down, ln2_g, ln2_b):
    batch, seq_len, d_model = x.shape
    depth = w_ada.shape[0]
    n = batch * seq_len
    alpha = (2 * depth) ** 0.25
    d_ff = w_down.shape[1]
    ffp = _round_up(d_ff, 512)
    tm_proj = min(1024, seq_len)
    tm = min(512, seq_len)

    mod = _adaln_mod(c, w_ada, b_ada)
    xf = x.reshape(n, d_model)
    for l in range(depth):
        sh_m, sc_m, g_m, sh_f, sc_f, g_f = (
            mod[l, :batch, t * d_model:(t + 1) * d_model].reshape(batch, 1, d_model) for t in range(N_MOD))
        w_main, w_side, ssm_width = _split_w_in(w_in[l])
        proj = _modulated_matmul(xf, sc_m, sh_m, w_main, BF16, seq_len, tm_proj, 512, "in_proj")
        side = _modulated_matmul(xf, sc_m, sh_m, w_side, F32, seq_len, tm_proj, w_side.shape[1], "in_proj_side")

        sb_o = _sb_attention(proj, batch, seq_len, 3 * DSA_HEADS, 3 * DSA_HEADS + SB_HEADS,
                             3 * DSA_HEADS + 2 * SB_HEADS)
        side_blk = ssm_width // LANE
        dsa_cols = {"dq": 0, "dk": 1, "dv": 2,
                    "iq": (3 * DSA_WIDTH + 3 * SB_WIDTH + ssm_width) // (IDX_HEADS * IDX_DIM),
                    "ik_even": side_blk, "ik_odd": side_blk + 1, "iw": side_blk + 2}
        dsa_o = _dsa_attention(proj, side, rel_bias, batch, seq_len, dsa_cols)

        groups = ssm_width // SSM_GROUP
        u_chunks = jnp.transpose(
            side[:, :ssm_width].reshape(n // SSM_CHUNK, SSM_CHUNK, groups, SSM_GROUP), (2, 0, 1, 3)
        ).reshape(groups, n // SSM_CHUNK, SSM_CHUNK * SSM_GROUP)
        ops = _s5_operators(ssm_a_re[l], ssm_a_im[l], ssm_log_dt[l], ssm_b_re[l], ssm_b_im[l],
                            ssm_c_re[l], ssm_c_im[l])
        y_chunks = _s5_scan(u_chunks, ops, batch)
        y_scan = jnp.transpose(
            y_chunks.reshape(groups, n // SSM_CHUNK, SSM_CHUNK, SSM_GROUP), (1, 2, 0, 3)).reshape(n, ssm_width)
        ssm_o = _s5_glu(y_scan, side, ssm_d[l], w_glu[l], b_glu[l], tm)

        x1, u_f = _outproj_ln(sb_o, ssm_o, dsa_o, w_out[l].astype(BF16), xf, g_m, ln1_g[l], ln1_b[l],
                              sc_f, sh_f, seq_len, alpha, tm)

        pad_cols = lambda a: jnp.pad(a, ((0, 0), (0, ffp - d_ff)))
        w_up_p = jnp.concatenate([pad_cols(w_up[l][:, :d_ff]), pad_cols(w_up[l][:, d_ff:])], axis=1).astype(BF16)
        conv_w_p = jnp.concatenate([pad_cols(conv_w[l][:, :d_ff]), pad_cols(conv_w[l][:, d_ff:])], axis=1)
        conv_b_p = jnp.concatenate([pad_cols(conv_b[l][None, :d_ff]), pad_cols(conv_b[l][None, d_ff:])], axis=1)
        w_down_p = jnp.pad(w_down[l], ((0, ffp - d_ff), (0, 0))).astype(BF16)
        h = _ffn_up(u_f, w_up_p, conv_w_p, conv_b_p, seq_len, tm, 512)
        xf = _ffn_down_ln(h, w_down_p, x1, g_f, ln2_g[l], ln2_b[l], seq_len, alpha, tm)
    return xf.reshape(batch, seq_len, d_model)
```

```python
import functools
import math

import jax
import jax.numpy as jnp
from jax import lax
from jax.experimental import pallas as pl
from jax.experimental.pallas import tpu as pltpu

F32 = jnp.float32
BF16 = jnp.bfloat16
I32 = jnp.int32

CHUNK = 64
CHUNK_SHIFT = CHUNK.bit_length() - 1
assert 1 << CHUNK_SHIFT == CHUNK
SB_HEADS = 6
DSA_HEADS = 6
HEAD_DIM = 128
SB_WIDTH = SB_HEADS * HEAD_DIM
DSA_WIDTH = DSA_HEADS * HEAD_DIM
SSM_GROUP = 16
SSM_STATE = 64
IDX_HEADS = 16
IDX_DIM = 64
DSA_MAX_TOPK = 256
REL_BUCKETS = 32
REL_MAX_DIST = 128
CONV_WIDTH = 3
N_MOD = 6
LN_EPS = 1e-5

LANE = 128
VMEM_LIMIT_BYTES = 56 * 1024 * 1024

SSM_CHUNK = 16
SB_BLOCK = 256
SB_HEAD_GROUP = 3
SB_ZERO_LOG = 104.0
DSA_TQ = 128
DSA_TK = 256
DSA_SUPER = 4
KEY_NEG_INF = -2139095041
INT_MIN = -2147483648
MASKED_LOGIT = -1e30


def _params(n_axes):
    return pltpu.CompilerParams(dimension_semantics=("arbitrary",) * n_axes,
                                vmem_limit_bytes=VMEM_LIMIT_BYTES)


def _dot(a, b):
    return jnp.dot(a, b, preferred_element_type=F32)


def _dot_nt(a, b):
    return lax.dot_general(a, b, (((1,), (1,)), ((), ())), preferred_element_type=F32)


def _split_bf16(a):
    hi = a.astype(BF16)
    lo = (a - hi.astype(F32)).astype(BF16)
    return hi, lo


def _dot3(a, b):
    ah, al = _split_bf16(a)
    bh, bl = _split_bf16(b)
    return _dot(ah, bh) + _dot(ah, bl) + _dot(al, bh)


def _layer_norm(y, g, b):
    mu = jnp.mean(y, axis=-1, keepdims=True)
    d = y - mu
    var = jnp.mean(d * d, axis=-1, keepdims=True)
    return d * lax.rsqrt(var + LN_EPS) * g + b


def _mod_kernel(c_ref, w_ref, b_ref, o_ref):
    c = c_ref[...]
    cond = c * jax.nn.sigmoid(c)
    o_ref[...] = _dot(cond.astype(BF16), w_ref[...].astype(BF16)) + b_ref[...]


def _adaln_mod(c, w_ada, b_ada):
    depth, d_model, n_out = w_ada.shape
    rows = 8
    c_pad = jnp.zeros((rows, d_model), F32).at[: c.shape[0]].set(c)
    tn = 1024
    return pl.pallas_call(
        _mod_kernel,
        grid=(depth, n_out // tn),
        in_specs=[
            pl.BlockSpec((rows, d_model), lambda l, j: (0, 0)),
            pl.BlockSpec((None, d_model, tn), lambda l, j: (l, 0, j)),
            pl.BlockSpec((None, 1, tn), lambda l, j: (l, 0, j)),
        ],
        out_specs=pl.BlockSpec((None, rows, tn), lambda l, j: (l, 0, j)),
        out_shape=jax.ShapeDtypeStruct((depth, rows, n_out), F32),
        compiler_params=_params(2),
        name="adaln_mod",
    )(c_pad, w_ada, b_ada.reshape(depth, 1, n_out))


def _inproj_kernel(x_ref, sc_ref, sh_ref, w_ref, o_ref, u_ref):
    @pl.when(pl.program_id(1) == 0)
    def _():
        u_ref[...] = (x_ref[...] * (1.0 + sc_ref[...]) + sh_ref[...]).astype(BF16)

    o_ref[...] = _dot(u_ref[...], w_ref[...]).astype(o_ref.dtype)


def _modulated_matmul(x, scale, shift, w, out_dtype, seq_len, tm, tn, name):
    n, d = x.shape
    ncols = w.shape[1]
    return pl.pallas_call(
        _inproj_kernel,
        grid=(n // tm, ncols // tn),
        in_specs=[
            pl.BlockSpec((tm, d), lambda i, j: (i, 0)),
            pl.BlockSpec((None, 1, d), lambda i, j: ((i * tm) // seq_len, 0, 0)),
            pl.BlockSpec((None, 1, d), lambda i, j: ((i * tm) // seq_len, 0, 0)),
            pl.BlockSpec((d, tn), lambda i, j: (0, j)),
        ],
        out_specs=pl.BlockSpec((tm, tn), lambda i, j: (i, j)),
        out_shape=jax.ShapeDtypeStruct((n, ncols), out_dtype),
        scratch_shapes=[pltpu.VMEM((tm, d), BF16)],
        compiler_params=_params(2),
        name=name,
    )(x, scale, shift, w)


def _sb_kernel(q_ref, k_ref, v_ref, o_ref, acc_ref, run_ref, *, blk, scale, heads):
    i = pl.program_id(2)
    rows = lax.broadcasted_iota(I32, (blk, blk), 0)
    cols = lax.broadcasted_iota(I32, (blk, blk), 1)
    tri = jnp.where(rows > cols, 1.0, 0.0).astype(BF16)
    keep = cols < rows

    def block(kb, masked):
        start = pl.multiple_of(kb * blk, blk)
        head_cols = [slice(h * HEAD_DIM, (h + 1) * HEAD_DIM) for h in range(heads)]
        log_sig, log_1m, split = [], [], []
        for hs in head_cols:
            z = _dot_nt(q_ref[:, hs], k_ref[pl.ds(start, blk), hs]) * scale
            softplus = jnp.maximum(z, 0.0) + jnp.log(1.0 + jnp.exp(-jnp.abs(z)))
            l1m = -softplus
            if masked:
                l1m = jnp.where(keep, l1m, 0.0)
            log_sig.append(z - softplus)
            log_1m.append(l1m)
            split.append(_split_bf16(l1m))
        suffix = [_dot(hi, tri) + _dot(lo, tri) for hi, lo in split]
        tops = []
        for h, hs in enumerate(head_cols):
            run = run_ref[h]
            w = jnp.exp(log_sig[h] + suffix[h] + run)
            if masked:
                w = jnp.where(keep, w, 0.0)
            acc_ref[h] += _dot(w.astype(BF16), v_ref[pl.ds(start, blk), hs])
            run_new = run + jnp.sum(log_1m[h], axis=1, keepdims=True)
            run_ref[h] = run_new
            tops.append(jnp.max(run_new))
        return functools.reduce(jnp.maximum, tops)

    acc_ref[...] = jnp.zeros_like(acc_ref)
    run_ref[...] = jnp.zeros_like(run_ref)
    top = block(i, True)

    def cond(carry):
        kb, top = carry
        return jnp.logical_and(kb >= 0, top > -SB_ZERO_LOG)

    def body(carry):
        kb, _ = carry
        return kb - 1, block(kb, False)

    lax.while_loop(cond, body, (i - 1, top))
    for h in range(heads):
        o_ref[:, h * HEAD_DIM:(h + 1) * HEAD_DIM] = acc_ref[h].astype(o_ref.dtype)


def _sb_attention(proj, batch, seq_len, q_col, k_col, v_col):
    n = proj.shape[0]
    blk = min(SB_BLOCK, seq_len)
    nq = seq_len // blk
    hg = SB_HEAD_GROUP
    assert SB_HEADS % hg == 0 and q_col % hg == 0 and k_col % hg == 0 and v_col % hg == 0
    width = hg * HEAD_DIM
    kern = functools.partial(_sb_kernel, blk=blk, scale=HEAD_DIM ** -0.5, heads=hg)
    return pl.pallas_call(
        kern,
        grid=(batch, SB_HEADS // hg, nq),
        in_specs=[
            pl.BlockSpec((blk, width), lambda b, g, i: (b * nq + i, q_col // hg + g)),
            pl.BlockSpec((seq_len, width), lambda b, g, i: (b, k_col // hg + g)),
            pl.BlockSpec((seq_len, width), lambda b, g, i: (b, v_col // hg + g)),
        ],
        out_specs=pl.BlockSpec((blk, width), lambda b, g, i: (b * nq + i, g)),
        out_shape=jax.ShapeDtypeStruct((n, SB_WIDTH), BF16),
        scratch_shapes=[pltpu.VMEM((hg, blk, HEAD_DIM), F32), pltpu.VMEM((hg, blk, 1), F32)],
        compiler_params=_params(3),
        name="sb_attention",
    )(proj, proj, proj)


def _t5_bucket(rel):
    half = REL_BUCKETS // 2
    max_exact = half // 2
    n = jnp.abs(rel)
    nf = jnp.maximum(n, 1).astype(F32)
    large = max_exact + (jnp.log(nf / max_exact) / math.log(REL_MAX_DIST / max_exact)
                         * (half - max_exact)).astype(I32)
    large = jnp.minimum(large, half - 1)
    return jnp.where(rel > 0, half, 0) + jnp.where(n < max_exact, n, large)


def _dsa_kernel(rb_ref, qd_ref, kd_ref, vd_ref, iq_ref, ika_ref, ikb_ref, iw_ref, o_ref,
                ka_s, kb_s, wb_s, keys_s, bias_s, acc_s, m_s, l_s, j_s, prefix_s, cnt_s, mask_s, logit_s,
                *, tq, tk, sup, topk, seq_len, scale):
    i = pl.program_id(1)
    q0 = i * tq
    n_kb = (q0 + tq + tk - 1) // tk
    bias_w = bias_s.shape[2]
    bias_shift = bias_w - tk

    @pl.when(i == 0)
    def _setup():
        ka_s[...] = ika_ref[...].astype(BF16)
        kb_s[...] = ikb_ref[...].astype(BF16)
        r = lax.broadcasted_iota(I32, (tq, bias_w), 0)
        c = lax.broadcasted_iota(I32, (tq, bias_w), 1)
        bucket = _t5_bucket(c - r - bias_shift)
        for h in range(DSA_HEADS):
            bias = jnp.zeros((tq, bias_w), F32)
            for j in range(REL_BUCKETS):
                bias = jnp.where(bucket == j, rb_ref[j, h], bias)
            bias_s[h] = bias * (1.0 / math.log(2.0))

    iw = iw_ref[...]
    for h in range(IDX_HEADS):
        wb_s[h] = jnp.broadcast_to(iw[:, h:h + 1] * (IDX_DIM ** -0.5 * IDX_HEADS ** -0.5), (tq, LANE))

    rows = lax.broadcasted_iota(I32, (tq, tk), 0)
    cols = lax.broadcasted_iota(I32, (tq, tk), 1)
    q_chunk = (q0 + rows) >> CHUNK_SHIFT

    def grouped(n, body, carry):
        def group(g, c):
            for u in range(sup):
                c = body(g * sup + u, c)
            return c
        carry = lax.fori_loop(0, n // sup, group, carry)
        return lax.fori_loop(n // sup * sup, n, body, carry)

    def score_block(kb, carry):
        start = pl.multiple_of(kb * tk, tk)
        k_even = ka_s[pl.ds(start, tk), :]
        k_odd = kb_s[pl.ds(start, tk), :]
        halves = [jnp.zeros((tq, LANE), F32) for _ in range(tk // LANE)]
        for pair in range(IDX_HEADS // 2):
            lhs = iq_ref[:, pair * LANE:(pair + 1) * LANE]
            for kk, h in ((k_even, 2 * pair), (k_odd, 2 * pair + 1)):
                d = _dot_nt(lhs, kk)
                w = wb_s[h]
                for s in range(tk // LANE):
                    halves[s] = halves[s] + jnp.maximum(d[:, s * LANE:(s + 1) * LANE], 0.0) * w
        score = jnp.concatenate(halves, axis=1)
        score = jnp.where(score == 0.0, 0.0, score)
        admissible = ((start + cols) >> CHUNK_SHIFT) <= q_chunk
        score = jnp.where(admissible, score, -jnp.inf)
        bits = pltpu.bitcast(score, I32)
        keys_s[:, pl.ds(start, tk)] = jnp.where(bits < 0, bits ^ 0x7FFFFFFF, bits)
        return carry

    grouped(n_kb, score_block, 0)

    def pad_block(kb, carry):
        keys_s[:, pl.ds(pl.multiple_of(kb * tk, tk), tk)] = jnp.full((tq, tk), KEY_NEG_INF, I32)
        return carry

    lax.fori_loop(n_kb, (n_kb + sup - 1) // sup * sup, pad_block, 0)

    lane_pos = lax.broadcasted_iota(I32, (tq, LANE), 1)

    def count(pred):
        def body(kb, acc):
            for s in range(tk // LANE):
                first = pl.multiple_of(kb * tk + s * LANE, LANE)
                acc = acc + jnp.where(pred(keys_s[:, pl.ds(first, LANE)], first), 1.0, 0.0)
            return acc
        acc = grouped(n_kb, body, jnp.zeros((tq, LANE), F32))
        return jnp.broadcast_to(jnp.sum(acc, axis=1, keepdims=True), (tq, LANE))

    prefix_s[...] = jnp.full((tq, LANE), INT_MIN, I32)
    cnt_s[...] = jnp.full((tq, LANE), tk, F32) * n_kb.astype(F32)

    def select_cond(carry):
        it, settled = carry
        return jnp.logical_and(it < 32, settled == 0)

    def select_bit(carry):
        it, _ = carry
        bit = 31 - it
        prefix = prefix_s[...]
        cnt_old = cnt_s[...]
        settled = jnp.max(jnp.abs(cnt_old - topk)) == 0.0
        cand = jnp.where(bit == 31, jnp.zeros_like(prefix), prefix | jnp.left_shift(1, bit))
        cnt = count(lambda k, first: k >= cand)
        take = cnt >= topk
        prefix_s[...] = jnp.where(take, cand, prefix)
        cnt_s[...] = jnp.where(take, cnt, cnt_old)
        return it + 1, settled.astype(I32)

    lax.while_loop(select_cond, select_bit, (jnp.int32(0), jnp.int32(0)))
    kth = prefix_s[...]

    tied = jnp.where(kth > KEY_NEG_INF, cnt_s[...] - topk, 0.0)
    j_s[...] = jnp.full((tq, LANE), seq_len, I32)

    @pl.when(jnp.max(tied) > 0.0)
    def _break_ties():
        need = topk - count(lambda k, first: k > kth)

        def pos_bit(it, last):
            bit = (seq_len - 1).bit_length() - 1 - it
            cand = last | jnp.left_shift(1, bit)
            cnt = count(lambda k, first: jnp.where((first + lane_pos) < cand, k, INT_MIN) == kth)
            return jnp.where(cnt < need, cand, last)
        j_s[...] = lax.fori_loop(0, (seq_len - 1).bit_length(), pos_bit, jnp.zeros((tq, LANE), I32))

    last_tied = j_s[...]
    kth_eff = jnp.maximum(kth, KEY_NEG_INF + 1)

    log2e = 1.0 / math.log(2.0)
    m_s[...] = jnp.full_like(m_s, MASKED_LOGIT)
    l_s[...] = jnp.zeros_like(l_s)
    acc_s[...] = jnp.zeros_like(acc_s)
    n_sup = (n_kb + sup - 1) // sup

    def attend_super(sb, carry):
        base = pl.multiple_of(sb * (sup * tk), sup * tk)
        for t in range(sup * tk // LANE):
            first = pl.multiple_of(base + t * LANE, LANE)
            sel = keys_s[:, pl.ds(first, LANE)] >= jnp.where((first + lane_pos) > last_tied, kth_eff + 1, kth_eff)
            mask_s[:, t * LANE:(t + 1) * LANE] = jnp.where(sel, 0.0, -jnp.inf)
        def logits(h):
            hs = slice(h * HEAD_DIM, (h + 1) * HEAD_DIM)
            qh = qd_ref[:, hs]
            top = jnp.full((tq, LANE), -jnp.inf, F32)
            for j in range(sup):
                start = base + j * tk
                off = pl.multiple_of(jnp.clip(start - q0 + bias_shift, 0, bias_shift), LANE)
                s = (_dot_nt(qh, kd_ref[pl.ds(start, tk), hs]) * (scale * log2e)
                     + bias_s[h, :, pl.ds(off, tk)] + mask_s[:, j * tk:(j + 1) * tk])
                logit_s[h, :, j * tk:(j + 1) * tk] = s
                for t in range(tk // LANE):
                    top = jnp.maximum(top, s[:, t * LANE:(t + 1) * LANE])
            m_old = m_s[h]
            m_new = jnp.maximum(m_old, jnp.max(top, axis=1, keepdims=True))
            m_s[h] = m_new
            return m_new, jnp.exp2(m_old - m_new)

        def weigh(h, m_new, alpha):
            hs = slice(h * HEAD_DIM, (h + 1) * HEAD_DIM)
            pv = jnp.zeros((tq, HEAD_DIM), F32)
            psum = jnp.zeros((tq, LANE), F32)
            for j in range(sup):
                start = base + j * tk
                tiles = [jnp.exp2(logit_s[h, :, j * tk + t * LANE:j * tk + (t + 1) * LANE] - m_new)
                         for t in range(tk // LANE)]
                for p_t in tiles:
                    psum = psum + p_t
                p = jnp.concatenate([p_t.astype(BF16) for p_t in tiles], axis=1)
                pv = pv + _dot(p, vd_ref[pl.ds(start, tk), hs])
            acc_s[h] = alpha * acc_s[h] + pv
            l_s[h] = alpha * l_s[h] + psum

        nxt = logits(0)
        for h in range(DSA_HEADS):
            cur = nxt
            if h + 1 < DSA_HEADS:
                nxt = logits(h + 1)
            weigh(h, *cur)
        return carry

    lax.fori_loop(0, n_sup, attend_super, 0)
    for h in range(DSA_HEADS):
        denom = jnp.sum(l_s[h], axis=1, keepdims=True)
        o_ref[:, h * HEAD_DIM:(h + 1) * HEAD_DIM] = (acc_s[h] / denom).astype(o_ref.dtype)


def _dsa_attention(proj, side, rel_bias, batch, seq_len, cols):
    n = proj.shape[0]
    tq = min(DSA_TQ, seq_len)
    tk = min(DSA_TK, seq_len)
    nq = seq_len // tq
    topk = min(DSA_MAX_TOPK, seq_len // 4)
    bias_w = tk + 2 * tk
    sup = min(DSA_SUPER, seq_len // tk)
    kern = functools.partial(_dsa_kernel, tq=tq, tk=tk, sup=sup, topk=topk, seq_len=seq_len,
                             scale=HEAD_DIM ** -0.5)
    once = pl.Buffered(1)
    return pl.pallas_call(
        kern,
        grid=(batch, nq),
        in_specs=[
            pl.BlockSpec(memory_space=pltpu.SMEM),
            pl.BlockSpec((tq, DSA_WIDTH), lambda b, i: (b * nq + i, cols["dq"])),
            pl.BlockSpec((seq_len, DSA_WIDTH), lambda b, i: (b, cols["dk"]), pipeline_mode=once),
            pl.BlockSpec((seq_len, DSA_WIDTH), lambda b, i: (b, cols["dv"]), pipeline_mode=once),
            pl.BlockSpec((tq, IDX_HEADS * IDX_DIM), lambda b, i: (b * nq + i, cols["iq"])),
            pl.BlockSpec((seq_len, LANE), lambda b, i: (b, cols["ik_even"]), pipeline_mode=once),
            pl.BlockSpec((seq_len, LANE), lambda b, i: (b, cols["ik_odd"]), pipeline_mode=once),
            pl.BlockSpec((tq, LANE), lambda b, i: (b * nq + i, cols["iw"])),
        ],
        out_specs=pl.BlockSpec((tq, DSA_WIDTH), lambda b, i: (b * nq + i, 0)),
        out_shape=jax.ShapeDtypeStruct((n, DSA_WIDTH), BF16),
        scratch_shapes=[
            pltpu.VMEM((seq_len, LANE), BF16),
            pltpu.VMEM((seq_len, LANE), BF16),
            pltpu.VMEM((IDX_HEADS, tq, LANE), F32),
            pltpu.VMEM((tq, seq_len), I32),
            pltpu.VMEM((DSA_HEADS, tq, bias_w), F32),
            pltpu.VMEM((DSA_HEADS, tq, HEAD_DIM), F32),
            pltpu.VMEM((DSA_HEADS, tq, LANE), F32),
            pltpu.VMEM((DSA_HEADS, tq, LANE), F32),
            pltpu.VMEM((tq, LANE), I32),
            pltpu.VMEM((tq, LANE), I32),
            pltpu.VMEM((tq, LANE), F32),
            pltpu.VMEM((tq, sup * tk), F32),
            pltpu.VMEM((DSA_HEADS, tq, sup * tk), F32),
        ],
        compiler_params=_params(2),
        name="dsa_attention",
    )(rel_bias, proj, proj, proj, proj, side, side, side)


def _s5_operators(a_re, a_im, log_dt, b_re, b_im, c_re, c_im):
    hp = lax.Precision.HIGHEST
    L = SSM_CHUNK
    ar, ai = a_re.astype(F32), a_im.astype(F32)
    dt = jnp.exp(log_dt.astype(F32))[:, None]
    mag = jnp.exp(dt * ar)
    abar_re, abar_im = mag * jnp.cos(dt * ai), mag * jnp.sin(dt * ai)
    den = ar * ar + ai * ai
    nr = abar_re - 1.0
    f_re = (nr * ar + abar_im * ai) / den
    f_im = (abar_im * ar - nr * ai) / den
    br, bi = b_re.astype(F32), b_im.astype(F32)
    bb_re = f_re[..., None] * br - f_im[..., None] * bi
    bb_im = f_re[..., None] * bi + f_im[..., None] * br
    pw_re, pw_im = [jnp.ones_like(abar_re)], [jnp.zeros_like(abar_im)]
    for _ in range(L):
        pr, pi = pw_re[-1], pw_im[-1]
        pw_re.append(pr * abar_re - pi * abar_im)
        pw_im.append(pr * abar_im + pi * abar_re)
    pw_re, pw_im = jnp.stack(pw_re, 1), jnp.stack(pw_im, 1)
    cr, ci = c_re.astype(F32), c_im.astype(F32)
    cp_re = cr[:, None] * pw_re[:, :, None, :] - ci[:, None] * pw_im[:, :, None, :]
    cp_im = cr[:, None] * pw_im[:, :, None, :] + ci[:, None] * pw_re[:, :, None, :]
    kern = (jnp.einsum('gtop,gpi->gtoi', cp_re[:, :L], bb_re, precision=hp)
            - jnp.einsum('gtop,gpi->gtoi', cp_im[:, :L], bb_im, precision=hp))
    s_idx = jnp.arange(L)[:, None]
    t_idx = jnp.arange(L)[None, :]
    lag = t_idx - s_idx
    toep = jnp.where((lag >= 0)[None, :, :, None, None], kern[:, jnp.clip(lag, 0, L - 1)], 0.0)
    G = ar.shape[0]
    hc = SSM_GROUP
    mt = jnp.transpose(toep, (0, 1, 4, 2, 3)).reshape(G, L * hc, L * hc)
    rev_re, rev_im = pw_re[:, L - 1::-1][:, :L], pw_im[:, L - 1::-1][:, :L]
    w_re = (rev_re[:, :, None, :] * jnp.swapaxes(bb_re, 1, 2)[:, None]
            - rev_im[:, :, None, :] * jnp.swapaxes(bb_im, 1, 2)[:, None]).reshape(G, L * hc, SSM_STATE)
    w_im = (rev_re[:, :, None, :] * jnp.swapaxes(bb_im, 1, 2)[:, None]
            + rev_im[:, :, None, :] * jnp.swapaxes(bb_re, 1, 2)[:, None]).reshape(G, L * hc, SSM_STATE)
    v_re = jnp.transpose(cp_re[:, 1:], (0, 3, 1, 2)).reshape(G, SSM_STATE, L * hc)
    v_im = -jnp.transpose(cp_im[:, 1:], (0, 3, 1, 2)).reshape(G, SSM_STATE, L * hc)
    lam = jnp.stack([pw_re[:, L], pw_im[:, L]], axis=1)
    return mt, w_re, w_im, v_re, v_im, lam


def _s5_kernel(u_ref, mt_ref, wre_ref, wim_ref, vre_ref, vim_ref, lam_ref, y_ref,
               sre_s, sim_s, hre_s, him_s, *, batch):
    u = u_ref[...]
    sre_s[...] = _dot3(u, wre_ref[...])
    sim_s[...] = _dot3(u, wim_ref[...])
    lre = lam_ref[0:1, :]
    lim = lam_ref[1:2, :]
    n_chunks = u.shape[0] // batch
    state_w = lre.shape[1]

    def step(c, carry):
        out = []
        for b in range(batch):
            hr, hi = carry[b]
            row = b * n_chunks + c
            hre_s[pl.ds(row, 1), :] = hr
            him_s[pl.ds(row, 1), :] = hi
            sr = sre_s[pl.ds(row, 1), :]
            si = sim_s[pl.ds(row, 1), :]
            out.append((lre * hr - lim * hi + sr, lre * hi + lim * hr + si))
        return tuple(out)

    zero = jnp.zeros((1, state_w), F32)
    lax.fori_loop(0, n_chunks, step, tuple((zero, zero) for _ in range(batch)))
    y_ref[...] = (_dot3(u, mt_ref[...]) + _dot3(hre_s[...], vre_ref[...])
                  + _dot3(him_s[...], vim_ref[...]))


def _s5_scan(u_chunks, ops, batch):
    mt, w_re, w_im, v_re, v_im, lam = ops
    G, nc, cw = u_chunks.shape
    P = SSM_STATE
    grp = lambda *shape: pl.BlockSpec((None,) + shape, lambda g: (g,) + (0,) * len(shape))
    return pl.pallas_call(
        functools.partial(_s5_kernel, batch=batch),
        grid=(G,),
        in_specs=[grp(nc, cw), grp(cw, cw), grp(cw, P), grp(cw, P), grp(P, cw), grp(P, cw), grp(2, P)],
        out_specs=grp(nc, cw),
        out_shape=jax.ShapeDtypeStruct((G, nc, cw), F32),
        scratch_shapes=[pltpu.VMEM((nc, P), F32)] * 4,
        compiler_params=_params(1),
        name="s5_scan",
    )(u_chunks, mt, w_re, w_im, v_re, v_im, lam)


def _glu_kernel(y_ref, u_ref, d_ref, w_ref, b_ref, o_ref):
    y = y_ref[...] + d_ref[...] * u_ref[...]
    y = 0.5 * y * (1.0 + jnp.tanh(math.sqrt(2.0 / math.pi) * (y + 0.044715 * (y * y * y))))
    gate = jax.nn.sigmoid(_dot(y.astype(BF16), w_ref[...]) + b_ref[...])
    o_ref[...] = (y * gate).astype(o_ref.dtype)


def _s5_glu(y_scan, side, d_skip, w_glu, b_glu, tm):
    n, width = y_scan.shape
    row = lambda: pl.BlockSpec((1, width), lambda i: (0, 0))
    return pl.pallas_call(
        _glu_kernel,
        grid=(n // tm,),
        in_specs=[
            pl.BlockSpec((tm, width), lambda i: (i, 0)),
            pl.BlockSpec((tm, width), lambda i: (i, 0)),
            row(),
            pl.BlockSpec((width, width), lambda i: (0, 0)),
            row(),
        ],
        out_specs=pl.BlockSpec((tm, width), lambda i: (i, 0)),
        out_shape=jax.ShapeDtypeStruct((n, width), BF16),
        compiler_params=_params(1),
        name="s5_glu",
    )(y_scan, side, d_skip.reshape(1, width), w_glu.astype(BF16), b_glu.reshape(1, width))


def _outproj_kernel(sb_ref, ssm_ref, dsa_ref, w_ref, x_ref, gate_ref, g_ref, b_ref, sc_ref, sh_ref,
                    x_out_ref, u_out_ref, *, alpha):
    k0 = sb_ref.shape[1]
    k1 = k0 + ssm_ref.shape[1]
    mix = (_dot(sb_ref[...], w_ref[0:k0, :]) + _dot(ssm_ref[...], w_ref[k0:k1, :])
           + _dot(dsa_ref[...], w_ref[k1:, :]))
    y = alpha * x_ref[...] + (1.0 + gate_ref[...]) * mix
    x1 = _layer_norm(y, g_ref[...], b_ref[...])
    x_out_ref[...] = x1
    u_out_ref[...] = (x1 * (1.0 + sc_ref[...]) + sh_ref[...]).astype(u_out_ref.dtype)


def _outproj_ln(sb_o, ssm_o, dsa_o, w_out, x, gate, ln_g, ln_b, scale, shift, seq_len, alpha, tm):
    n, d = x.shape
    per_batch = lambda: pl.BlockSpec((None, 1, d), lambda i: ((i * tm) // seq_len, 0, 0))
    vec = lambda: pl.BlockSpec((1, d), lambda i: (0, 0))
    tile = lambda w: pl.BlockSpec((tm, w), lambda i: (i, 0))
    return pl.pallas_call(
        functools.partial(_outproj_kernel, alpha=alpha),
        grid=(n // tm,),
        in_specs=[tile(sb_o.shape[1]), tile(ssm_o.shape[1]), tile(dsa_o.shape[1]),
                  pl.BlockSpec(w_out.shape, lambda i: (0, 0)),
                  tile(d), per_batch(), vec(), vec(), per_batch(), per_batch()],
        out_specs=[tile(d), tile(d)],
        out_shape=[jax.ShapeDtypeStruct((n, d), F32), jax.ShapeDtypeStruct((n, d), BF16)],
        compiler_params=_params(1),
        name="outproj_ln",
    )(sb_o, ssm_o, dsa_o, w_out, x, gate, ln_g.reshape(1, d), ln_b.reshape(1, d), scale, shift)


def _ffn_up_kernel(u_ref, halo_ref, wg_ref, wv_ref, cwg_ref, cwv_ref, cbg_ref, cbv_ref, o_ref,
                   *scratch, tm, halo, seq_len, sub):
    i = pl.program_id(0)
    has_prev = jnp.where((i * tm) % seq_len == 0, 0.0, 1.0)
    u = u_ref[...]
    uh = halo_ref[...]

    def conv(a_s, w_ref, cw_ref, cb_ref, cs):
        a_s[0:halo, :] = _dot(uh, w_ref[:, cs]) * has_prev
        a_s[halo:, :] = _dot(u, w_ref[:, cs])
        out = cb_ref[:, cs]
        for tap in range(CONV_WIDTH):
            back = CONV_WIDTH - 1 - tap
            out = out + cw_ref[tap:tap + 1, cs] * a_s[halo - back:halo - back + tm, :]
        return out

    for c in range(len(scratch) // 2):
        cs = slice(c * sub, c * sub + scratch[2 * c].shape[1])
        g = conv(scratch[2 * c], wg_ref, cwg_ref, cbg_ref, cs)
        val = conv(scratch[2 * c + 1], wv_ref, cwv_ref, cbv_ref, cs)
        o_ref[:, cs] = (g * jax.nn.sigmoid(g) * val).astype(o_ref.dtype)


def _ffn_up(u, w_up, conv_w, conv_b, seq_len, tm, tf):
    n, d = u.shape
    ffp = w_up.shape[1] // 2
    nj = ffp // tf
    halo = 16
    sub = 2 * LANE
    return pl.pallas_call(
        functools.partial(_ffn_up_kernel, tm=tm, halo=halo, seq_len=seq_len, sub=sub),
        grid=(n // tm, nj),
        in_specs=[
            pl.BlockSpec((tm, d), lambda i, j: (i, 0)),
            pl.BlockSpec((halo, d), lambda i, j: (jnp.maximum(i * (tm // halo) - 1, 0), 0)),
            pl.BlockSpec((d, tf), lambda i, j: (0, j)),
            pl.BlockSpec((d, tf), lambda i, j: (0, j + nj)),
            pl.BlockSpec((CONV_WIDTH, tf), lambda i, j: (0, j)),
            pl.BlockSpec((CONV_WIDTH, tf), lambda i, j: (0, j + nj)),
            pl.BlockSpec((1, tf), lambda i, j: (0, j)),
            pl.BlockSpec((1, tf), lambda i, j: (0, j + nj)),
        ],
        out_specs=pl.BlockSpec((tm, tf), lambda i, j: (i, j)),
        out_shape=jax.ShapeDtypeStruct((n, ffp), BF16),
        scratch_shapes=[pltpu.VMEM((tm + halo, min(sub, tf - c * sub)), F32)
                        for c in range(pl.cdiv(tf, sub)) for _ in range(2)],
        compiler_params=_params(2),
        name="ffn_up_conv_gate",
    )(u, u, w_up, w_up, conv_w, conv_w, conv_b, conv_b)


def _ffn_down_kernel(h_ref, w_ref, x_ref, gate_ref, g_ref, b_ref, o_ref, *, alpha):
    y = alpha * x_ref[...] + (1.0 + gate_ref[...]) * _dot(h_ref[...], w_ref[...])
    o_ref[...] = _layer_norm(y, g_ref[...], b_ref[...])


def _ffn_down_ln(h, w_down, x, gate, ln_g, ln_b, seq_len, alpha, tm):
    n, d = x.shape
    ffp = h.shape[1]
    return pl.pallas_call(
        functools.partial(_ffn_down_kernel, alpha=alpha),
        grid=(n // tm,),
        in_specs=[
            pl.BlockSpec((tm, ffp), lambda i: (i, 0)),
            pl.BlockSpec((ffp, d), lambda i: (0, 0), pipeline_mode=pl.Buffered(1)),
            pl.BlockSpec((tm, d), lambda i: (i, 0)),
            pl.BlockSpec((None, 1, d), lambda i: ((i * tm) // seq_len, 0, 0)),
            pl.BlockSpec((1, d), lambda i: (0, 0)),
            pl.BlockSpec((1, d), lambda i: (0, 0)),
        ],
        out_specs=pl.BlockSpec((tm, d), lambda i: (i, 0)),
        out_shape=jax.ShapeDtypeStruct((n, d), F32),
        compiler_params=_params(1),
        name="ffn_down_ln",
    )(h, w_down, x, gate, ln_g.reshape(1, d), ln_b.reshape(1, d))


def _round_up(v, m):
    return (v + m - 1) // m * m


def _split_w_in(w_in_l):
    sizes = (SB_WIDTH, SB_WIDTH, SB_WIDTH, w_in_l.shape[1] - 3 * SB_WIDTH - 3 * DSA_WIDTH
             - IDX_HEADS * IDX_DIM - IDX_DIM - IDX_HEADS, DSA_WIDTH, DSA_WIDTH, DSA_WIDTH,
             IDX_HEADS * IDX_DIM, IDX_DIM, IDX_HEADS)
    offs = [0]
    for s in sizes:
        offs.append(offs[-1] + s)
    sbq, sbk, sbv, ssm, dq, dk, dv, iq, ik, iw = (w_in_l[:, offs[t]:offs[t + 1]] for t in range(10))
    d = w_in_l.shape[0]
    z = lambda w: jnp.zeros((d, w), w_in_l.dtype)
    main = jnp.concatenate([dq, dk, dv, sbq, sbk, sbv, ssm, iq], axis=1).astype(BF16)
    side = jnp.concatenate([ssm, ik, z(LANE - IDX_DIM), z(LANE - IDX_DIM), ik, iw, z(LANE - IDX_HEADS)],
                           axis=1).astype(BF16)
    return main, side, sizes[3]


def kernel(x, c, w_ada, b_ada, w_in, w_out, ssm_a_re, ssm_a_im, ssm_log_dt, ssm_b_re, ssm_b_im,
           ssm_c_re, ssm_c_im, ssm_d, w_glu, b_glu, rel_bias, ln1_g, ln1_b, w_up, conv_w, conv_b,
           w_down, ln2_g, ln2_b):
    batch, seq_len, d_model = x.shape
    depth = w_ada.shape[0]
    n = batch * seq_len
    alpha = (2 * depth) ** 0.25
    d_ff = w_down.shape[1]
    ffp = _round_up(d_ff, 512)
    tm_proj = min(1024, seq_len)
    tm = min(512, seq_len)

    mod = _adaln_mod(c, w_ada, b_ada)
    xf = x.reshape(n, d_model)
    for l in range(depth):
        sh_m, sc_m, g_m, sh_f, sc_f, g_f = (
            mod[l, :batch, t * d_model:(t + 1) * d_model].reshape(batch, 1, d_model) for t in range(N_MOD))
        w_main, w_side, ssm_width = _split_w_in(w_in[l])
        proj = _modulated_matmul(xf, sc_m, sh_m, w_main, BF16, seq_len, tm_proj, 512, "in_proj")
        side = _modulated_matmul(xf, sc_m, sh_m, w_side, F32, seq_len, tm_proj, w_side.shape[1], "in_proj_side")

        sb_o = _sb_attention(proj, batch, seq_len, 3 * DSA_HEADS, 3 * DSA_HEADS + SB_HEADS,
                             3 * DSA_HEADS + 2 * SB_HEADS)
        side_blk = ssm_width // LANE
        dsa_cols = {"dq": 0, "dk": 1, "dv": 2,
                    "iq": (3 * DSA_WIDTH + 3 * SB_WIDTH + ssm_width) // (IDX_HEADS * IDX_DIM),
                    "ik_even": side_blk, "ik_odd": side_blk + 1, "iw": side_blk + 2}
        dsa_o = _dsa_attention(proj, side, rel_bias, batch, seq_len, dsa_cols)

        groups = ssm_width // SSM_GROUP
        u_chunks = jnp.transpose(
            side[:, :ssm_width].reshape(n // SSM_CHUNK, SSM_CHUNK, groups, SSM_GROUP), (2, 0, 1, 3)
        ).reshape(groups, n // SSM_CHUNK, SSM_CHUNK * SSM_GROUP)
        ops = _s5_operators(ssm_a_re[l], ssm_a_im[l], ssm_log_dt[l], ssm_b_re[l], ssm_b_im[l],
                            ssm_c_re[l], ssm_c_im[l])
        y_chunks = _s5_scan(u_chunks, ops, batch)
        y_scan = jnp.transpose(
            y_chunks.reshape(groups, n // SSM_CHUNK, SSM_CHUNK, SSM_GROUP), (1, 2, 0, 3)).reshape(n, ssm_width)
        ssm_o = _s5_glu(y_scan, side, ssm_d[l], w_glu[l], b_glu[l], tm)

        x1, u_f = _outproj_ln(sb_o, ssm_o, dsa_o, w_out[l].astype(BF16), xf, g_m, ln1_g[l], ln1_b[l],
                              sc_f, sh_f, seq_len, alpha, tm)

        pad_cols = lambda a: jnp.pad(a, ((0, 0), (0, ffp - d_ff)))
        w_up_p = jnp.concatenate([pad_cols(w_up[l][:, :d_ff]), pad_cols(w_up[l][:, d_ff:])], axis=1).astype(BF16)
        conv_w_p = jnp.concatenate([pad_cols(conv_w[l][:, :d_ff]), pad_cols(conv_w[l][:, d_ff:])], axis=1)
        conv_b_p = jnp.concatenate([pad_cols(conv_b[l][None, :d_ff]), pad_cols(conv_b[l][None, d_ff:])], axis=1)
        w_down_p = jnp.pad(w_down[l], ((0, ffp - d_ff), (0, 0))).astype(BF16)
        h = _ffn_up(u_f, w_up_p, conv_w_p, conv_b_p, seq_len, tm, 512)
        xf = _ffn_down_ln(h, w_down_p, x1, g_f, ln2_g[l], ln2_b[l], seq_len, alpha, tm)
    return xf.reshape(batch, seq_len, d_model)
```

```python
import functools
import math

import jax
import jax.numpy as jnp
from jax import lax
from jax.experimental import pallas as pl
from jax.experimental.pallas import tpu as pltpu

F32 = jnp.float32
BF16 = jnp.bfloat16
I32 = jnp.int32

CHUNK = 64
CHUNK_SHIFT = CHUNK.bit_length() - 1
assert 1 << CHUNK_SHIFT == CHUNK
SB_HEADS = 6
DSA_HEADS = 6
HEAD_DIM = 128
SB_WIDTH = SB_HEADS * HEAD_DIM
DSA_WIDTH = DSA_HEADS * HEAD_DIM
SSM_GROUP = 16
SSM_STATE = 64
IDX_HEADS = 16
IDX_DIM = 64
DSA_MAX_TOPK = 256
REL_BUCKETS = 32
REL_MAX_DIST = 128
CONV_WIDTH = 3
N_MOD = 6
LN_EPS = 1e-5

LANE = 128
VMEM_LIMIT_BYTES = 56 * 1024 * 1024

SSM_CHUNK = 16
SB_BLOCK = 256
SB_HEAD_GROUP = 3
SB_ZERO_LOG = 104.0
DSA_TQ = 128
DSA_TK = 256
DSA_SUPER = 4
KEY_NEG_INF = -2139095041
INT_MIN = -2147483648
MASKED_LOGIT = -1e30


def _params(n_axes):
    return pltpu.CompilerParams(dimension_semantics=("arbitrary",) * n_axes,
                                vmem_limit_bytes=VMEM_LIMIT_BYTES)


def _dot(a, b):
    return jnp.dot(a, b, preferred_element_type=F32)


def _dot_nt(a, b):
    return lax.dot_general(a, b, (((1,), (1,)), ((), ())), preferred_element_type=F32)


def _split_bf16(a):
    hi = a.astype(BF16)
    lo = (a - hi.astype(F32)).astype(BF16)
    return hi, lo


def _dot3(a, b):
    ah, al = _split_bf16(a)
    bh, bl = _split_bf16(b)
    return _dot(ah, bh) + _dot(ah, bl) + _dot(al, bh)


def _layer_norm(y, g, b):
    mu = jnp.mean(y, axis=-1, keepdims=True)
    d = y - mu
    var = jnp.mean(d * d, axis=-1, keepdims=True)
    return d * lax.rsqrt(var + LN_EPS) * g + b


def _mod_kernel(c_ref, w_ref, b_ref, o_ref):
    c = c_ref[...]
    cond = c * jax.nn.sigmoid(c)
    o_ref[...] = _dot(cond.astype(BF16), w_ref[...].astype(BF16)) + b_ref[...]


def _adaln_mod(c, w_ada, b_ada):
    depth, d_model, n_out = w_ada.shape
    rows = 8
    c_pad = jnp.zeros((rows, d_model), F32).at[: c.shape[0]].set(c)
    tn = 1024
    return pl.pallas_call(
        _mod_kernel,
        grid=(depth, n_out // tn),
        in_specs=[
            pl.BlockSpec((rows, d_model), lambda l, j: (0, 0)),
            pl.BlockSpec((None, d_model, tn), lambda l, j: (l, 0, j)),
            pl.BlockSpec((None, 1, tn), lambda l, j: (l, 0, j)),
        ],
        out_specs=pl.BlockSpec((None, rows, tn), lambda l, j: (l, 0, j)),
        out_shape=jax.ShapeDtypeStruct((depth, rows, n_out), F32),
        compiler_params=_params(2),
        name="adaln_mod",
    )(c_pad, w_ada, b_ada.reshape(depth, 1, n_out))


def _inproj_kernel(x_ref, sc_ref, sh_ref, w_ref, o_ref, u_ref):
    @pl.when(pl.program_id(1) == 0)
    def _():
        u_ref[...] = (x_ref[...] * (1.0 + sc_ref[...]) + sh_ref[...]).astype(BF16)

    o_ref[...] = _dot(u_ref[...], w_ref[...]).astype(o_ref.dtype)


def _modulated_matmul(x, scale, shift, w, layer, out_dtype, seq_len, tm, tn, name):
    n, d = x.shape
    ncols = w.shape[2]
    return pl.pallas_call(
        _inproj_kernel,
        grid=(n // tm, ncols // tn),
        in_specs=[
            pl.BlockSpec((tm, d), lambda i, j: (i, 0)),
            pl.BlockSpec((None, 1, d), lambda i, j: ((i * tm) // seq_len, 0, 0)),
            pl.BlockSpec((None, 1, d), lambda i, j: ((i * tm) // seq_len, 0, 0)),
            pl.BlockSpec((None, d, tn), lambda i, j: (layer, 0, j)),
        ],
        out_specs=pl.BlockSpec((tm, tn), lambda i, j: (i, j)),
        out_shape=jax.ShapeDtypeStruct((n, ncols), out_dtype),
        scratch_shapes=[pltpu.VMEM((tm, d), BF16)],
        compiler_params=_params(2),
        name=name,
    )(x, scale, shift, w)


def _sb_kernel(q_ref, k_ref, v_ref, o_ref, acc_ref, run_ref, *, blk, scale, heads):
    i = pl.program_id(2)
    rows = lax.broadcasted_iota(I32, (blk, blk), 0)
    cols = lax.broadcasted_iota(I32, (blk, blk), 1)
    tri = jnp.where(rows > cols, 1.0, 0.0).astype(BF16)
    keep = cols < rows

    def block(kb, masked):
        start = pl.multiple_of(kb * blk, blk)
        head_cols = [slice(h * HEAD_DIM, (h + 1) * HEAD_DIM) for h in range(heads)]
        log_sig, log_1m, split = [], [], []
        for hs in head_cols:
            z = _dot_nt(q_ref[:, hs], k_ref[pl.ds(start, blk), hs]) * scale
            softplus = jnp.maximum(z, 0.0) + jnp.log(1.0 + jnp.exp(-jnp.abs(z)))
            l1m = -softplus
            if masked:
                l1m = jnp.where(keep, l1m, 0.0)
            log_sig.append(z - softplus)
            log_1m.append(l1m)
            split.append(_split_bf16(l1m))
        suffix = [_dot(hi, tri) + _dot(lo, tri) for hi, lo in split]
        tops = []
        for h, hs in enumerate(head_cols):
            run = run_ref[h]
            w = jnp.exp(log_sig[h] + suffix[h] + run)
            if masked:
                w = jnp.where(keep, w, 0.0)
            acc_ref[h] += _dot(w.astype(BF16), v_ref[pl.ds(start, blk), hs])
            run_new = run + jnp.sum(log_1m[h], axis=1, keepdims=True)
            run_ref[h] = run_new
            tops.append(jnp.max(run_new))
        return functools.reduce(jnp.maximum, tops)

    acc_ref[...] = jnp.zeros_like(acc_ref)
    run_ref[...] = jnp.zeros_like(run_ref)
    top = block(i, True)

    def cond(carry):
        kb, top = carry
        return jnp.logical_and(kb >= 0, top > -SB_ZERO_LOG)

    def body(carry):
        kb, _ = carry
        return kb - 1, block(kb, False)

    lax.while_loop(cond, body, (i - 1, top))
    for h in range(heads):
        o_ref[:, h * HEAD_DIM:(h + 1) * HEAD_DIM] = acc_ref[h].astype(o_ref.dtype)


def _sb_attention(proj, batch, seq_len, q_col, k_col, v_col):
    n = proj.shape[0]
    blk = min(SB_BLOCK, seq_len)
    nq = seq_len // blk
    hg = SB_HEAD_GROUP
    assert SB_HEADS % hg == 0 and q_col % hg == 0 and k_col % hg == 0 and v_col % hg == 0
    width = hg * HEAD_DIM
    kern = functools.partial(_sb_kernel, blk=blk, scale=HEAD_DIM ** -0.5, heads=hg)
    return pl.pallas_call(
        kern,
        grid=(batch, SB_HEADS // hg, nq),
        in_specs=[
            pl.BlockSpec((blk, width), lambda b, g, i: (b * nq + i, q_col // hg + g)),
            pl.BlockSpec((seq_len, width), lambda b, g, i: (b, k_col // hg + g)),
            pl.BlockSpec((seq_len, width), lambda b, g, i: (b, v_col // hg + g)),
        ],
        out_specs=pl.BlockSpec((blk, width), lambda b, g, i: (b * nq + i, g)),
        out_shape=jax.ShapeDtypeStruct((n, SB_WIDTH), BF16),
        scratch_shapes=[pltpu.VMEM((hg, blk, HEAD_DIM), F32), pltpu.VMEM((hg, blk, 1), F32)],
        compiler_params=_params(3),
        name="sb_attention",
    )(proj, proj, proj)


def _t5_bucket(rel):
    half = REL_BUCKETS // 2
    max_exact = half // 2
    n = jnp.abs(rel)
    nf = jnp.maximum(n, 1).astype(F32)
    large = max_exact + (jnp.log(nf / max_exact) / math.log(REL_MAX_DIST / max_exact)
                         * (half - max_exact)).astype(I32)
    large = jnp.minimum(large, half - 1)
    return jnp.where(rel > 0, half, 0) + jnp.where(n < max_exact, n, large)


def _dsa_kernel(rb_ref, qd_ref, kd_ref, vd_ref, iq_ref, ika_ref, ikb_ref, iw_ref, o_ref,
                ka_s, kb_s, wb_s, keys_s, bias_s, acc_s, m_s, l_s, j_s, prefix_s, cnt_s, mask_s, logit_s,
                *, tq, tk, sup, topk, seq_len, scale):
    i = pl.program_id(1)
    q0 = i * tq
    n_kb = (q0 + tq + tk - 1) // tk
    bias_w = bias_s.shape[2]
    bias_shift = bias_w - tk

    @pl.when(i == 0)
    def _setup():
        ka_s[...] = ika_ref[...].astype(BF16)
        kb_s[...] = ikb_ref[...].astype(BF16)
        r = lax.broadcasted_iota(I32, (tq, bias_w), 0)
        c = lax.broadcasted_iota(I32, (tq, bias_w), 1)
        bucket = _t5_bucket(c - r - bias_shift)
        for h in range(DSA_HEADS):
            bias = jnp.zeros((tq, bias_w), F32)
            for j in range(REL_BUCKETS):
                bias = jnp.where(bucket == j, rb_ref[j, h], bias)
            bias_s[h] = bias * (1.0 / math.log(2.0))

    iw = iw_ref[...]
    for h in range(IDX_HEADS):
        wb_s[h] = jnp.broadcast_to(iw[:, h:h + 1] * (IDX_DIM ** -0.5 * IDX_HEADS ** -0.5), (tq, LANE))

    rows = lax.broadcasted_iota(I32, (tq, tk), 0)
    cols = lax.broadcasted_iota(I32, (tq, tk), 1)
    q_chunk = (q0 + rows) >> CHUNK_SHIFT

    def grouped(n, body, carry):
        def group(g, c):
            for u in range(sup):
                c = body(g * sup + u, c)
            return c
        carry = lax.fori_loop(0, n // sup, group, carry)
        return lax.fori_loop(n // sup * sup, n, body, carry)

    def score_block(kb, carry):
        start = pl.multiple_of(kb * tk, tk)
        k_even = ka_s[pl.ds(start, tk), :]
        k_odd = kb_s[pl.ds(start, tk), :]
        halves = [jnp.zeros((tq, LANE), F32) for _ in range(tk // LANE)]
        for pair in range(IDX_HEADS // 2):
            lhs = iq_ref[:, pair * LANE:(pair + 1) * LANE]
            for kk, h in ((k_even, 2 * pair), (k_odd, 2 * pair + 1)):
                d = _dot_nt(lhs, kk)
                w = wb_s[h]
                for s in range(tk // LANE):
                    halves[s] = halves[s] + jnp.maximum(d[:, s * LANE:(s + 1) * LANE], 0.0) * w
        score = jnp.concatenate(halves, axis=1)
        score = jnp.where(score == 0.0, 0.0, score)
        admissible = ((start + cols) >> CHUNK_SHIFT) <= q_chunk
        score = jnp.where(admissible, score, -jnp.inf)
        bits = pltpu.bitcast(score, I32)
        keys_s[:, pl.ds(start, tk)] = jnp.where(bits < 0, bits ^ 0x7FFFFFFF, bits)
        return carry

    grouped(n_kb, score_block, 0)

    def pad_block(kb, carry):
        keys_s[:, pl.ds(pl.multiple_of(kb * tk, tk), tk)] = jnp.full((tq, tk), KEY_NEG_INF, I32)
        return carry

    lax.fori_loop(n_kb, (n_kb + sup - 1) // sup * sup, pad_block, 0)

    lane_pos = lax.broadcasted_iota(I32, (tq, LANE), 1)

    def count(pred):
        def body(kb, acc):
            for s in range(tk // LANE):
                first = pl.multiple_of(kb * tk + s * LANE, LANE)
                acc = acc + jnp.where(pred(keys_s[:, pl.ds(first, LANE)], first), 1.0, 0.0)
            return acc
        acc = grouped(n_kb, body, jnp.zeros((tq, LANE), F32))
        return jnp.broadcast_to(jnp.sum(acc, axis=1, keepdims=True), (tq, LANE))

    prefix_s[...] = jnp.full((tq, LANE), INT_MIN, I32)
    cnt_s[...] = jnp.full((tq, LANE), tk, F32) * n_kb.astype(F32)

    def select_cond(carry):
        it, settled = carry
        return jnp.logical_and(it < 32, settled == 0)

    def select_bit(carry):
        it, _ = carry
        bit = 31 - it
        prefix = prefix_s[...]
        cnt_old = cnt_s[...]
        settled = jnp.max(jnp.abs(cnt_old - topk)) == 0.0
        cand = jnp.where(bit == 31, jnp.zeros_like(prefix), prefix | jnp.left_shift(1, bit))
        cnt = count(lambda k, first: k >= cand)
        take = cnt >= topk
        prefix_s[...] = jnp.where(take, cand, prefix)
        cnt_s[...] = jnp.where(take, cnt, cnt_old)
        return it + 1, settled.astype(I32)

    lax.while_loop(select_cond, select_bit, (jnp.int32(0), jnp.int32(0)))
    kth = prefix_s[...]

    tied = jnp.where(kth > KEY_NEG_INF, cnt_s[...] - topk, 0.0)
    j_s[...] = jnp.full((tq, LANE), seq_len, I32)

    @pl.when(jnp.max(tied) > 0.0)
    def _break_ties():
        need = topk - count(lambda k, first: k > kth)

        def pos_bit(it, last):
            bit = (seq_len - 1).bit_length() - 1 - it
            cand = last | jnp.left_shift(1, bit)
            cnt = count(lambda k, first: jnp.where((first + lane_pos) < cand, k, INT_MIN) == kth)
            return jnp.where(cnt < need, cand, last)
        j_s[...] = lax.fori_loop(0, (seq_len - 1).bit_length(), pos_bit, jnp.zeros((tq, LANE), I32))

    last_tied = j_s[...]
    kth_eff = jnp.maximum(kth, KEY_NEG_INF + 1)

    log2e = 1.0 / math.log(2.0)
    m_s[...] = jnp.full_like(m_s, MASKED_LOGIT)
    l_s[...] = jnp.zeros_like(l_s)
    acc_s[...] = jnp.zeros_like(acc_s)
    n_sup = (n_kb + sup - 1) // sup

    def attend_super(sb, carry):
        base = pl.multiple_of(sb * (sup * tk), sup * tk)
        for t in range(sup * tk // LANE):
            first = pl.multiple_of(base + t * LANE, LANE)
            sel = keys_s[:, pl.ds(first, LANE)] >= jnp.where((first + lane_pos) > last_tied, kth_eff + 1, kth_eff)
            mask_s[:, t * LANE:(t + 1) * LANE] = jnp.where(sel, 0.0, -jnp.inf)
        def logits(h):
            hs = slice(h * HEAD_DIM, (h + 1) * HEAD_DIM)
            qh = qd_ref[:, hs]
            top = jnp.full((tq, LANE), -jnp.inf, F32)
            for j in range(sup):
                start = base + j * tk
                off = pl.multiple_of(jnp.clip(start - q0 + bias_shift, 0, bias_shift), LANE)
                s = (_dot_nt(qh, kd_ref[pl.ds(start, tk), hs]) * (scale * log2e)
                     + bias_s[h, :, pl.ds(off, tk)] + mask_s[:, j * tk:(j + 1) * tk])
                logit_s[h, :, j * tk:(j + 1) * tk] = s
                for t in range(tk // LANE):
                    top = jnp.maximum(top, s[:, t * LANE:(t + 1) * LANE])
            m_old = m_s[h]
            m_new = jnp.maximum(m_old, jnp.max(top, axis=1, keepdims=True))
            m_s[h] = m_new
            return m_new, jnp.exp2(m_old - m_new)

        def weigh(h, m_new, alpha):
            hs = slice(h * HEAD_DIM, (h + 1) * HEAD_DIM)
            pv = jnp.zeros((tq, HEAD_DIM), F32)
            psum = jnp.zeros((tq, LANE), F32)
            for j in range(sup):
                start = base + j * tk
                tiles = [jnp.exp2(logit_s[h, :, j * tk + t * LANE:j * tk + (t + 1) * LANE] - m_new)
                         for t in range(tk // LANE)]
                for p_t in tiles:
                    psum = psum + p_t
                p = jnp.concatenate([p_t.astype(BF16) for p_t in tiles], axis=1)
                pv = pv + _dot(p, vd_ref[pl.ds(start, tk), hs])
            acc_s[h] = alpha * acc_s[h] + pv
            l_s[h] = alpha * l_s[h] + psum

        nxt = logits(0)
        for h in range(DSA_HEADS):
            cur = nxt
            if h + 1 < DSA_HEADS:
                nxt = logits(h + 1)
            weigh(h, *cur)
        return carry

    lax.fori_loop(0, n_sup, attend_super, 0)
    for h in range(DSA_HEADS):
        denom = jnp.sum(l_s[h], axis=1, keepdims=True)
        o_ref[:, h * HEAD_DIM:(h + 1) * HEAD_DIM] = (acc_s[h] / denom).astype(o_ref.dtype)


def _dsa_attention(proj, side, rel_bias, batch, seq_len, cols):
    n = proj.shape[0]
    tq = min(DSA_TQ, seq_len)
    tk = min(DSA_TK, seq_len)
    nq = seq_len // tq
    topk = min(DSA_MAX_TOPK, seq_len // 4)
    bias_w = tk + 2 * tk
    sup = min(DSA_SUPER, seq_len // tk)
    kern = functools.partial(_dsa_kernel, tq=tq, tk=tk, sup=sup, topk=topk, seq_len=seq_len,
                             scale=HEAD_DIM ** -0.5)
    once = pl.Buffered(1)
    return pl.pallas_call(
        kern,
        grid=(batch, nq),
        in_specs=[
            pl.BlockSpec(memory_space=pltpu.SMEM),
            pl.BlockSpec((tq, DSA_WIDTH), lambda b, i: (b * nq + i, cols["dq"])),
            pl.BlockSpec((seq_len, DSA_WIDTH), lambda b, i: (b, cols["dk"]), pipeline_mode=once),
            pl.BlockSpec((seq_len, DSA_WIDTH), lambda b, i: (b, cols["dv"]), pipeline_mode=once),
            pl.BlockSpec((tq, IDX_HEADS * IDX_DIM), lambda b, i: (b * nq + i, cols["iq"])),
            pl.BlockSpec((seq_len, LANE), lambda b, i: (b, cols["ik_even"]), pipeline_mode=once),
            pl.BlockSpec((seq_len, LANE), lambda b, i: (b, cols["ik_odd"]), pipeline_mode=once),
            pl.BlockSpec((tq, LANE), lambda b, i: (b * nq + i, cols["iw"])),
        ],
        out_specs=pl.BlockSpec((tq, DSA_WIDTH), lambda b, i: (b * nq + i, 0)),
        out_shape=jax.ShapeDtypeStruct((n, DSA_WIDTH), BF16),
        scratch_shapes=[
            pltpu.VMEM((seq_len, LANE), BF16),
            pltpu.VMEM((seq_len, LANE), BF16),
            pltpu.VMEM((IDX_HEADS, tq, LANE), F32),
            pltpu.VMEM((tq, seq_len), I32),
            pltpu.VMEM((DSA_HEADS, tq, bias_w), F32),
            pltpu.VMEM((DSA_HEADS, tq, HEAD_DIM), F32),
            pltpu.VMEM((DSA_HEADS, tq, LANE), F32),
            pltpu.VMEM((DSA_HEADS, tq, LANE), F32),
            pltpu.VMEM((tq, LANE), I32),
            pltpu.VMEM((tq, LANE), I32),
            pltpu.VMEM((tq, LANE), F32),
            pltpu.VMEM((tq, sup * tk), F32),
            pltpu.VMEM((DSA_HEADS, tq, sup * tk), F32),
        ],
        compiler_params=_params(2),
        name="dsa_attention",
    )(rel_bias, proj, proj, proj, proj, side, side, side)


def _s5_operators(a_re, a_im, log_dt, b_re, b_im, c_re, c_im):
    hp = lax.Precision.HIGHEST
    L = SSM_CHUNK
    ar, ai = a_re.astype(F32), a_im.astype(F32)
    dt = jnp.exp(log_dt.astype(F32))[:, None]
    mag = jnp.exp(dt * ar)
    abar_re, abar_im = mag * jnp.cos(dt * ai), mag * jnp.sin(dt * ai)
    den = ar * ar + ai * ai
    nr = abar_re - 1.0
    f_re = (nr * ar + abar_im * ai) / den
    f_im = (abar_im * ar - nr * ai) / den
    br, bi = b_re.astype(F32), b_im.astype(F32)
    bb_re = f_re[..., None] * br - f_im[..., None] * bi
    bb_im = f_re[..., None] * bi + f_im[..., None] * br
    pw_re, pw_im = [jnp.ones_like(abar_re)], [jnp.zeros_like(abar_im)]
    for _ in range(L):
        pr, pi = pw_re[-1], pw_im[-1]
        pw_re.append(pr * abar_re - pi * abar_im)
        pw_im.append(pr * abar_im + pi * abar_re)
    pw_re, pw_im = jnp.stack(pw_re, 1), jnp.stack(pw_im, 1)
    cr, ci = c_re.astype(F32), c_im.astype(F32)
    cp_re = cr[:, None] * pw_re[:, :, None, :] - ci[:, None] * pw_im[:, :, None, :]
    cp_im = cr[:, None] * pw_im[:, :, None, :] + ci[:, None] * pw_re[:, :, None, :]
    kern = (jnp.einsum('gtop,gpi->gtoi', cp_re[:, :L], bb_re, precision=hp)
            - jnp.einsum('gtop,gpi->gtoi', cp_im[:, :L], bb_im, precision=hp))
    s_idx = jnp.arange(L)[:, None]
    t_idx = jnp.arange(L)[None, :]
    lag = t_idx - s_idx
    toep = jnp.where((lag >= 0)[None, :, :, None, None], kern[:, jnp.clip(lag, 0, L - 1)], 0.0)
    G = ar.shape[0]
    hc = SSM_GROUP
    mt = jnp.transpose(toep, (0, 1, 4, 2, 3)).reshape(G, L * hc, L * hc)
    rev_re, rev_im = pw_re[:, L - 1::-1][:, :L], pw_im[:, L - 1::-1][:, :L]
    w_re = (rev_re[:, :, None, :] * jnp.swapaxes(bb_re, 1, 2)[:, None]
            - rev_im[:, :, None, :] * jnp.swapaxes(bb_im, 1, 2)[:, None]).reshape(G, L * hc, SSM_STATE)
    w_im = (rev_re[:, :, None, :] * jnp.swapaxes(bb_im, 1, 2)[:, None]
            + rev_im[:, :, None, :] * jnp.swapaxes(bb_re, 1, 2)[:, None]).reshape(G, L * hc, SSM_STATE)
    v_re = jnp.transpose(cp_re[:, 1:], (0, 3, 1, 2)).reshape(G, SSM_STATE, L * hc)
    v_im = -jnp.transpose(cp_im[:, 1:], (0, 3, 1, 2)).reshape(G, SSM_STATE, L * hc)
    lam = jnp.stack([pw_re[:, L], pw_im[:, L]], axis=1)
    return mt, w_re, w_im, v_re, v_im, lam


def _to_chunks_kernel(*refs, hc):
    *u_refs, o_ref = refs
    nc = o_ref.shape[1]
    per_tile = LANE // hc
    for j, u_ref in enumerate(u_refs):
        steps = [u_ref[pl.ds(s, nc, stride=SSM_CHUNK), :].T for s in range(SSM_CHUNK)]
        for gl in range(per_tile):
            o_ref[j * per_tile + gl] = jnp.concatenate(
                [st[gl * hc:(gl + 1) * hc, :] for st in steps], axis=0).T


def _from_chunks_kernel(y_ref, o_ref, *tile_s, hc):
    nc = y_ref.shape[1]
    per_tile = LANE // hc
    for j, t_s in enumerate(tile_s):
        per_group = [y_ref[j * per_tile + gl].T for gl in range(per_tile)]
        for s in range(SSM_CHUNK):
            t_s[pl.ds(s, nc, stride=SSM_CHUNK), :] = jnp.concatenate(
                [yg[s * hc:(s + 1) * hc, :] for yg in per_group], axis=0).T
        o_ref[:, j * LANE:(j + 1) * LANE] = t_s[...]


def _s5_to_chunks(side, width, groups, rows):
    n = side.shape[0]
    nc = rows // SSM_CHUNK
    hc = width // groups
    tiles = width // LANE
    return pl.pallas_call(
        functools.partial(_to_chunks_kernel, hc=hc),
        grid=(n // rows,),
        in_specs=[pl.BlockSpec((rows, LANE), lambda i, j=j: (i, j)) for j in range(tiles)],
        out_specs=pl.BlockSpec((groups, nc, SSM_CHUNK * hc), lambda i: (0, i, 0)),
        out_shape=jax.ShapeDtypeStruct((groups, n // SSM_CHUNK, SSM_CHUNK * hc), F32),
        compiler_params=_params(1),
        name="s5_to_chunks",
    )(*([side] * tiles))


def _s5_from_chunks(y_chunks, rows):
    groups, n_chunks, cw = y_chunks.shape
    hc = cw // SSM_CHUNK
    width = groups * hc
    nc = rows // SSM_CHUNK
    return pl.pallas_call(
        functools.partial(_from_chunks_kernel, hc=hc),
        grid=(n_chunks // nc,),
        in_specs=[pl.BlockSpec((groups, nc, cw), lambda i: (0, i, 0))],
        out_specs=pl.BlockSpec((rows, width), lambda i: (i, 0)),
        out_shape=jax.ShapeDtypeStruct((n_chunks * SSM_CHUNK, width), F32),
        scratch_shapes=[pltpu.VMEM((rows, LANE), F32)] * (width // LANE),
        compiler_params=_params(1),
        name="s5_from_chunks",
    )(y_chunks)


def _s5_kernel(u_ref, mt_ref, wre_ref, wim_ref, vre_ref, vim_ref, lam_ref, y_ref,
               sre_s, sim_s, hre_s, him_s, *, batch):
    u = u_ref[...]
    sre_s[...] = _dot3(u, wre_ref[...])
    sim_s[...] = _dot3(u, wim_ref[...])
    lre = lam_ref[0:1, :]
    lim = lam_ref[1:2, :]
    n_chunks = u.shape[0] // batch
    state_w = lre.shape[1]

    def step(c, carry):
        out = []
        for b in range(batch):
            hr, hi = carry[b]
            row = b * n_chunks + c
            hre_s[pl.ds(row, 1), :] = hr
            him_s[pl.ds(row, 1), :] = hi
            sr = sre_s[pl.ds(row, 1), :]
            si = sim_s[pl.ds(row, 1), :]
            out.append((lre * hr - lim * hi + sr, lre * hi + lim * hr + si))
        return tuple(out)

    zero = jnp.zeros((1, state_w), F32)
    lax.fori_loop(0, n_chunks, step, tuple((zero, zero) for _ in range(batch)))
    y_ref[...] = (_dot3(u, mt_ref[...]) + _dot3(hre_s[...], vre_ref[...])
                  + _dot3(him_s[...], vim_ref[...]))


def _s5_scan(u_chunks, ops, batch):
    mt, w_re, w_im, v_re, v_im, lam = ops
    G, nc, cw = u_chunks.shape
    P = SSM_STATE
    grp = lambda *shape: pl.BlockSpec((None,) + shape, lambda g: (g,) + (0,) * len(shape))
    return pl.pallas_call(
        functools.partial(_s5_kernel, batch=batch),
        grid=(G,),
        in_specs=[grp(nc, cw), grp(cw, cw), grp(cw, P), grp(cw, P), grp(P, cw), grp(P, cw), grp(2, P)],
        out_specs=grp(nc, cw),
        out_shape=jax.ShapeDtypeStruct((G, nc, cw), F32),
        scratch_shapes=[pltpu.VMEM((nc, P), F32)] * 4,
        compiler_params=_params(1),
        name="s5_scan",
    )(u_chunks, mt, w_re, w_im, v_re, v_im, lam)


def _glu_kernel(y_ref, u_ref, d_ref, w_ref, b_ref, o_ref):
    y = y_ref[...] + d_ref[...] * u_ref[...]
    y = 0.5 * y * (1.0 + jnp.tanh(math.sqrt(2.0 / math.pi) * (y + 0.044715 * (y * y * y))))
    gate = jax.nn.sigmoid(_dot(y.astype(BF16), w_ref[...]) + b_ref[...])
    o_ref[...] = (y * gate).astype(o_ref.dtype)


def _s5_glu(y_scan, side, d_skip, w_glu, b_glu, tm):
    n, width = y_scan.shape
    row = lambda: pl.BlockSpec((1, width), lambda i: (0, 0))
    return pl.pallas_call(
        _glu_kernel,
        grid=(n // tm,),
        in_specs=[
            pl.BlockSpec((tm, width), lambda i: (i, 0)),
            pl.BlockSpec((tm, width), lambda i: (i, 0)),
            row(),
            pl.BlockSpec((width, width), lambda i: (0, 0)),
            row(),
        ],
        out_specs=pl.BlockSpec((tm, width), lambda i: (i, 0)),
        out_shape=jax.ShapeDtypeStruct((n, width), BF16),
        compiler_params=_params(1),
        name="s5_glu",
    )(y_scan, side, d_skip.reshape(1, width), w_glu.astype(BF16), b_glu.reshape(1, width))


def _outproj_kernel(sb_ref, ssm_ref, dsa_ref, w_ref, x_ref, gate_ref, g_ref, b_ref, sc_ref, sh_ref,
                    x_out_ref, u_out_ref, *, alpha):
    k0 = sb_ref.shape[1]
    k1 = k0 + ssm_ref.shape[1]
    mix = (_dot(sb_ref[...], w_ref[0:k0, :]) + _dot(ssm_ref[...], w_ref[k0:k1, :])
           + _dot(dsa_ref[...], w_ref[k1:, :]))
    y = alpha * x_ref[...] + (1.0 + gate_ref[...]) * mix
    x1 = _layer_norm(y, g_ref[...], b_ref[...])
    x_out_ref[...] = x1
    u_out_ref[...] = (x1 * (1.0 + sc_ref[...]) + sh_ref[...]).astype(u_out_ref.dtype)


def _outproj_ln(sb_o, ssm_o, dsa_o, w_out, layer, x, gate, ln_g, ln_b, scale, shift, seq_len, alpha, tm):
    n, d = x.shape
    per_batch = lambda: pl.BlockSpec((None, 1, d), lambda i: ((i * tm) // seq_len, 0, 0))
    vec = lambda: pl.BlockSpec((1, d), lambda i: (0, 0))
    tile = lambda w: pl.BlockSpec((tm, w), lambda i: (i, 0))
    return pl.pallas_call(
        functools.partial(_outproj_kernel, alpha=alpha),
        grid=(n // tm,),
        in_specs=[tile(sb_o.shape[1]), tile(ssm_o.shape[1]), tile(dsa_o.shape[1]),
                  pl.BlockSpec((None,) + w_out.shape[1:], lambda i: (layer, 0, 0)),
                  tile(d), per_batch(), vec(), vec(), per_batch(), per_batch()],
        out_specs=[tile(d), tile(d)],
        out_shape=[jax.ShapeDtypeStruct((n, d), F32), jax.ShapeDtypeStruct((n, d), BF16)],
        compiler_params=_params(1),
        name="outproj_ln",
    )(sb_o, ssm_o, dsa_o, w_out, x, gate, ln_g.reshape(1, d), ln_b.reshape(1, d), scale, shift)


def _ffn_up_kernel(u_ref, halo_ref, wg_ref, wv_ref, cwg_ref, cwv_ref, cbg_ref, cbv_ref, o_ref,
                   *, tm, halo, seq_len, sub):
    i = pl.program_id(0)
    has_prev = jnp.where((i * tm) % seq_len == 0, 0.0, 1.0)
    u = u_ref[...]
    uh = halo_ref[...]

    def conv(w_ref, cw_ref, cb_ref, cs):
        a = jnp.concatenate([_dot(uh, w_ref[:, cs]) * has_prev, _dot(u, w_ref[:, cs])], axis=0)
        out = cb_ref[:, cs] + cw_ref[CONV_WIDTH - 1:CONV_WIDTH, cs] * a[halo:, :]
        for back in range(1, CONV_WIDTH):
            tap = CONV_WIDTH - 1 - back
            out = out + cw_ref[tap:tap + 1, cs] * pltpu.roll(a, back, 0)[halo:, :]
        return out

    tf = o_ref.shape[1]
    for c in range(pl.cdiv(tf, sub)):
        cs = slice(c * sub, min((c + 1) * sub, tf))
        g = conv(wg_ref, cwg_ref, cbg_ref, cs)
        val = conv(wv_ref, cwv_ref, cbv_ref, cs)
        o_ref[:, cs] = (g * jax.nn.sigmoid(g) * val).astype(o_ref.dtype)


def _ffn_up(u, w_up, conv_w, conv_b, layer, seq_len, tm, tf):
    n, d = u.shape
    ffp = w_up.shape[2] // 2
    nj = ffp // tf
    halo = 16
    sub = 2 * LANE
    return pl.pallas_call(
        functools.partial(_ffn_up_kernel, tm=tm, halo=halo, seq_len=seq_len, sub=sub),
        grid=(n // tm, nj),
        in_specs=[
            pl.BlockSpec((tm, d), lambda i, j: (i, 0)),
            pl.BlockSpec((halo, d), lambda i, j: (jnp.maximum(i * (tm // halo) - 1, 0), 0)),
            pl.BlockSpec((None, d, tf), lambda i, j: (layer, 0, j)),
            pl.BlockSpec((None, d, tf), lambda i, j: (layer, 0, j + nj)),
            pl.BlockSpec((None, CONV_WIDTH, tf), lambda i, j: (layer, 0, j)),
            pl.BlockSpec((None, CONV_WIDTH, tf), lambda i, j: (layer, 0, j + nj)),
            pl.BlockSpec((None, 1, tf), lambda i, j: (layer, 0, j)),
            pl.BlockSpec((None, 1, tf), lambda i, j: (layer, 0, j + nj)),
        ],
        out_specs=pl.BlockSpec((tm, tf), lambda i, j: (i, j)),
        out_shape=jax.ShapeDtypeStruct((n, ffp), BF16),
        compiler_params=_params(2),
        name="ffn_up_conv_gate",
    )(u, u, w_up, w_up, conv_w, conv_w, conv_b, conv_b)


def _ffn_down_kernel(h_ref, w_ref, x_ref, gate_ref, g_ref, b_ref, o_ref, *, alpha):
    y = alpha * x_ref[...] + (1.0 + gate_ref[...]) * _dot(h_ref[...], w_ref[...])
    o_ref[...] = _layer_norm(y, g_ref[...], b_ref[...])


def _ffn_down_ln(h, w_down, layer, x, gate, ln_g, ln_b, seq_len, alpha, tm):
    n, d = x.shape
    ffp = h.shape[1]
    return pl.pallas_call(
        functools.partial(_ffn_down_kernel, alpha=alpha),
        grid=(n // tm,),
        in_specs=[
            pl.BlockSpec((tm, ffp), lambda i: (i, 0)),
            pl.BlockSpec((None, ffp, d), lambda i: (layer, 0, 0), pipeline_mode=pl.Buffered(1)),
            pl.BlockSpec((tm, d), lambda i: (i, 0)),
            pl.BlockSpec((None, 1, d), lambda i: ((i * tm) // seq_len, 0, 0)),
            pl.BlockSpec((1, d), lambda i: (0, 0)),
            pl.BlockSpec((1, d), lambda i: (0, 0)),
        ],
        out_specs=pl.BlockSpec((tm, d), lambda i: (i, 0)),
        out_shape=jax.ShapeDtypeStruct((n, d), F32),
        compiler_params=_params(1),
        name="ffn_down_ln",
    )(h, w_down, x, gate, ln_g.reshape(1, d), ln_b.reshape(1, d))


def _round_up(v, m):
    return (v + m - 1) // m * m


def _split_w_in(w_in):
    w = w_in.astype(BF16)
    sizes = (SB_WIDTH, SB_WIDTH, SB_WIDTH, w.shape[2] - 3 * SB_WIDTH - 3 * DSA_WIDTH
             - IDX_HEADS * IDX_DIM - IDX_DIM - IDX_HEADS, DSA_WIDTH, DSA_WIDTH, DSA_WIDTH,
             IDX_HEADS * IDX_DIM, IDX_DIM, IDX_HEADS)
    offs = [0]
    for s in sizes:
        offs.append(offs[-1] + s)
    sbq, sbk, sbv, ssm, dq, dk, dv, iq, ik, iw = (w[:, :, offs[t]:offs[t + 1]] for t in range(10))
    z = lambda width: jnp.zeros(w.shape[:2] + (width,), BF16)
    main = jnp.concatenate([dq, dk, dv, sbq, sbk, sbv, ssm, iq], axis=2)
    side = jnp.concatenate([ssm, ik, z(LANE - IDX_DIM), z(LANE - IDX_DIM), ik, iw, z(LANE - IDX_HEADS)],
                           axis=2)
    return main, side, sizes[3]


def _pad_halves(a, d_ff, ffp):
    lead = a.shape[:-1]
    halves = a.reshape(lead + (2, d_ff))
    halves = jnp.pad(halves, [(0, 0)] * (len(lead) + 1) + [(0, ffp - d_ff)])
    return halves.reshape(lead + (2 * ffp,))


def kernel(x, c, w_ada, b_ada, w_in, w_out, ssm_a_re, ssm_a_im, ssm_log_dt, ssm_b_re, ssm_b_im,
           ssm_c_re, ssm_c_im, ssm_d, w_glu, b_glu, rel_bias, ln1_g, ln1_b, w_up, conv_w, conv_b,
           w_down, ln2_g, ln2_b):
    batch, seq_len, d_model = x.shape
    depth = w_ada.shape[0]
    n = batch * seq_len
    alpha = (2 * depth) ** 0.25
    d_ff = w_down.shape[1]
    ffp = _round_up(d_ff, 512)
    tm_proj = min(1024, seq_len)
    tm = min(512, seq_len)

    mod = _adaln_mod(c, w_ada, b_ada)
    w_main, w_side, ssm_width = _split_w_in(w_in)
    w_out_b = w_out.astype(BF16)
    w_up_p = _pad_halves(w_up.astype(BF16), d_ff, ffp)
    conv_w_p = _pad_halves(conv_w, d_ff, ffp)
    conv_b_p = _pad_halves(conv_b[:, None, :], d_ff, ffp)
    w_down_p = jnp.pad(w_down.astype(BF16), ((0, 0), (0, ffp - d_ff), (0, 0)))
    relayout_rows = min(2048, seq_len)

    xf = x.reshape(n, d_model)
    for l in range(depth):
        sh_m, sc_m, g_m, sh_f, sc_f, g_f = (
            mod[l, :batch, t * d_model:(t + 1) * d_model].reshape(batch, 1, d_model) for t in range(N_MOD))
        proj = _modulated_matmul(xf, sc_m, sh_m, w_main, l, BF16, seq_len, tm_proj, 512, "in_proj")
        side = _modulated_matmul(xf, sc_m, sh_m, w_side, l, F32, seq_len, tm_proj, w_side.shape[2],
                                 "in_proj_side")

        sb_o = _sb_attention(proj, batch, seq_len, 3 * DSA_HEADS, 3 * DSA_HEADS + SB_HEADS,
                             3 * DSA_HEADS + 2 * SB_HEADS)
        side_blk = ssm_width // LANE
        dsa_cols = {"dq": 0, "dk": 1, "dv": 2,
                    "iq": (3 * DSA_WIDTH + 3 * SB_WIDTH + ssm_width) // (IDX_HEADS * IDX_DIM),
                    "ik_even": side_blk, "ik_odd": side_blk + 1, "iw": side_blk + 2}
        dsa_o = _dsa_attention(proj, side, rel_bias, batch, seq_len, dsa_cols)

        groups = ssm_width // SSM_GROUP
        u_chunks = _s5_to_chunks(side, ssm_width, groups, relayout_rows)
        ops = _s5_operators(ssm_a_re[l], ssm_a_im[l], ssm_log_dt[l], ssm_b_re[l], ssm_b_im[l],
                            ssm_c_re[l], ssm_c_im[l])
        y_chunks = _s5_scan(u_chunks, ops, batch)
        y_scan = _s5_from_chunks(y_chunks, relayout_rows)
        ssm_o = _s5_glu(y_scan, side, ssm_d[l], w_glu[l], b_glu[l], tm)

        x1, u_f = _outproj_ln(sb_o, ssm_o, dsa_o, w_out_b, l, xf, g_m, ln1_g[l], ln1_b[l],
                              sc_f, sh_f, seq_len, alpha, tm)
        h = _ffn_up(u_f, w_up_p, conv_w_p, conv_b_p, l, seq_len, tm, 512)
        xf = _ffn_down_ln(h, w_down_p, l, x1, g_f, ln2_g[l], ln2_b[l], seq_len, alpha, tm)
    return xf.reshape(batch, seq_len, d_model)
```

```python
import functools
import math

import jax
import jax.numpy as jnp
from jax import lax
from jax.experimental import pallas as pl
from jax.experimental.pallas import tpu as pltpu

F32 = jnp.float32
BF16 = jnp.bfloat16
I32 = jnp.int32

CHUNK = 64
CHUNK_SHIFT = CHUNK.bit_length() - 1
assert 1 << CHUNK_SHIFT == CHUNK
SB_HEADS = 6
DSA_HEADS = 6
HEAD_DIM = 128
SB_WIDTH = SB_HEADS * HEAD_DIM
DSA_WIDTH = DSA_HEADS * HEAD_DIM
SSM_GROUP = 16
SSM_STATE = 64
IDX_HEADS = 16
IDX_DIM = 64
DSA_MAX_TOPK = 256
REL_BUCKETS = 32
REL_MAX_DIST = 128
CONV_WIDTH = 3
N_MOD = 6
LN_EPS = 1e-5

LANE = 128
VMEM_LIMIT_BYTES = 56 * 1024 * 1024

SSM_CHUNK = 16
SB_BLOCK = 256
SB_HEAD_GROUP = 3
SB_ZERO_LOG = 104.0
DSA_TQ = 128
DSA_TK = 256
DSA_SUPER = 4
KEY_NEG_INF = -2139095041
INT_MIN = -2147483648
MASKED_LOGIT = -1e30


def _params(n_axes):
    return pltpu.CompilerParams(dimension_semantics=("arbitrary",) * n_axes,
                                vmem_limit_bytes=VMEM_LIMIT_BYTES)


def _dot(a, b):
    return jnp.dot(a, b, preferred_element_type=F32)


def _dot_nt(a, b):
    return lax.dot_general(a, b, (((1,), (1,)), ((), ())), preferred_element_type=F32)


def _split_bf16(a):
    hi = a.astype(BF16)
    lo = (a - hi.astype(F32)).astype(BF16)
    return hi, lo


def _dot3(a, b):
    ah, al = _split_bf16(a)
    bh, bl = _split_bf16(b)
    return _dot(ah, bh) + _dot(ah, bl) + _dot(al, bh)


def _layer_norm(y, g, b):
    mu = jnp.mean(y, axis=-1, keepdims=True)
    d = y - mu
    var = jnp.mean(d * d, axis=-1, keepdims=True)
    return d * lax.rsqrt(var + LN_EPS) * g + b


def _mod_kernel(c_ref, w_ref, b_ref, o_ref):
    c = c_ref[...]
    cond = c * jax.nn.sigmoid(c)
    o_ref[...] = _dot(cond.astype(BF16), w_ref[...].astype(BF16)) + b_ref[...]


def _adaln_mod(c, w_ada, b_ada):
    depth, d_model, n_out = w_ada.shape
    rows = 8
    c_pad = jnp.zeros((rows, d_model), F32).at[: c.shape[0]].set(c)
    tn = 1024
    return pl.pallas_call(
        _mod_kernel,
        grid=(depth, n_out // tn),
        in_specs=[
            pl.BlockSpec((rows, d_model), lambda l, j: (0, 0)),
            pl.BlockSpec((None, d_model, tn), lambda l, j: (l, 0, j)),
            pl.BlockSpec((None, 1, tn), lambda l, j: (l, 0, j)),
        ],
        out_specs=pl.BlockSpec((None, rows, tn), lambda l, j: (l, 0, j)),
        out_shape=jax.ShapeDtypeStruct((depth, rows, n_out), F32),
        compiler_params=_params(2),
        name="adaln_mod",
    )(c_pad, w_ada, b_ada.reshape(depth, 1, n_out))


def _inproj_kernel(x_ref, sc_ref, sh_ref, w_ref, o_ref, u_ref):
    @pl.when(pl.program_id(1) == 0)
    def _():
        u_ref[...] = (x_ref[...] * (1.0 + sc_ref[...]) + sh_ref[...]).astype(BF16)

    o_ref[...] = _dot(u_ref[...], w_ref[...]).astype(o_ref.dtype)


def _modulated_matmul(x, scale, shift, w, layer, out_dtype, seq_len, tm, tn, name):
    n, d = x.shape
    ncols = w.shape[2]
    return pl.pallas_call(
        _inproj_kernel,
        grid=(n // tm, ncols // tn),
        in_specs=[
            pl.BlockSpec((tm, d), lambda i, j: (i, 0)),
            pl.BlockSpec((None, 1, d), lambda i, j: ((i * tm) // seq_len, 0, 0)),
            pl.BlockSpec((None, 1, d), lambda i, j: ((i * tm) // seq_len, 0, 0)),
            pl.BlockSpec((None, d, tn), lambda i, j: (layer, 0, j)),
        ],
        out_specs=pl.BlockSpec((tm, tn), lambda i, j: (i, j)),
        out_shape=jax.ShapeDtypeStruct((n, ncols), out_dtype),
        scratch_shapes=[pltpu.VMEM((tm, d), BF16)],
        compiler_params=_params(2),
        name=name,
    )(x, scale, shift, w)


def _sb_kernel(q_ref, k_ref, v_ref, o_ref, acc_ref, run_ref, *, blk, scale, heads):
    i = pl.program_id(2)
    rows = lax.broadcasted_iota(I32, (blk, blk), 0)
    cols = lax.broadcasted_iota(I32, (blk, blk), 1)
    tri = jnp.where(rows > cols, 1.0, 0.0).astype(BF16)
    keep = cols < rows

    def block(kb, masked):
        start = pl.multiple_of(kb * blk, blk)
        head_cols = [slice(h * HEAD_DIM, (h + 1) * HEAD_DIM) for h in range(heads)]
        log_sig, log_1m, split = [], [], []
        for hs in head_cols:
            z = _dot_nt(q_ref[:, hs], k_ref[pl.ds(start, blk), hs]) * scale
            softplus = jnp.maximum(z, 0.0) + jnp.log(1.0 + jnp.exp(-jnp.abs(z)))
            l1m = -softplus
            if masked:
                l1m = jnp.where(keep, l1m, 0.0)
            log_sig.append(z - softplus)
            log_1m.append(l1m)
            split.append(_split_bf16(l1m))
        suffix = [_dot(hi, tri) + _dot(lo, tri) for hi, lo in split]
        tops = []
        for h, hs in enumerate(head_cols):
            run = run_ref[h]
            w = jnp.exp(log_sig[h] + suffix[h] + run)
            if masked:
                w = jnp.where(keep, w, 0.0)
            acc_ref[h] += _dot(w.astype(BF16), v_ref[pl.ds(start, blk), hs])
            run_new = run + jnp.sum(log_1m[h], axis=1, keepdims=True)
            run_ref[h] = run_new
            tops.append(jnp.max(run_new))
        return functools.reduce(jnp.maximum, tops)

    acc_ref[...] = jnp.zeros_like(acc_ref)
    run_ref[...] = jnp.zeros_like(run_ref)
    top = block(i, True)

    def cond(carry):
        kb, top = carry
        return jnp.logical_and(kb >= 0, top > -SB_ZERO_LOG)

    def body(carry):
        kb, _ = carry
        return kb - 1, block(kb, False)

    lax.while_loop(cond, body, (i - 1, top))
    for h in range(heads):
        o_ref[:, h * HEAD_DIM:(h + 1) * HEAD_DIM] = acc_ref[h].astype(o_ref.dtype)


def _sb_attention(proj, batch, seq_len, q_col, k_col, v_col):
    n = proj.shape[0]
    blk = min(SB_BLOCK, seq_len)
    nq = seq_len // blk
    hg = SB_HEAD_GROUP
    assert SB_HEADS % hg == 0 and q_col % hg == 0 and k_col % hg == 0 and v_col % hg == 0
    width = hg * HEAD_DIM
    kern = functools.partial(_sb_kernel, blk=blk, scale=HEAD_DIM ** -0.5, heads=hg)
    return pl.pallas_call(
        kern,
        grid=(batch, SB_HEADS // hg, nq),
        in_specs=[
            pl.BlockSpec((blk, width), lambda b, g, i: (b * nq + i, q_col // hg + g)),
            pl.BlockSpec((seq_len, width), lambda b, g, i: (b, k_col // hg + g)),
            pl.BlockSpec((seq_len, width), lambda b, g, i: (b, v_col // hg + g)),
        ],
        out_specs=pl.BlockSpec((blk, width), lambda b, g, i: (b * nq + i, g)),
        out_shape=jax.ShapeDtypeStruct((n, SB_WIDTH), BF16),
        scratch_shapes=[pltpu.VMEM((hg, blk, HEAD_DIM), F32), pltpu.VMEM((hg, blk, 1), F32)],
        compiler_params=_params(3),
        name="sb_attention",
    )(proj, proj, proj)


def _t5_bucket(rel):
    half = REL_BUCKETS // 2
    max_exact = half // 2
    n = jnp.abs(rel)
    nf = jnp.maximum(n, 1).astype(F32)
    large = max_exact + (jnp.log(nf / max_exact) / math.log(REL_MAX_DIST / max_exact)
                         * (half - max_exact)).astype(I32)
    large = jnp.minimum(large, half - 1)
    return jnp.where(rel > 0, half, 0) + jnp.where(n < max_exact, n, large)


def _dsa_kernel(rb_ref, qd_ref, kd_ref, vd_ref, iq_ref, ika_ref, ikb_ref, iw_ref, o_ref,
                ka_s, kb_s, wb_s, keys_s, bias_s, acc_s, m_s, l_s, j_s, prefix_s, cnt_s, mask_s, logit_s,
                *, tq, tk, sup, topk, seq_len, scale):
    i = pl.program_id(1)
    q0 = i * tq
    n_kb = (q0 + tq + tk - 1) // tk
    bias_w = bias_s.shape[2]
    bias_shift = bias_w - tk

    @pl.when(i == 0)
    def _setup():
        ka_s[...] = ika_ref[...].astype(BF16)
        kb_s[...] = ikb_ref[...].astype(BF16)
        r = lax.broadcasted_iota(I32, (tq, bias_w), 0)
        c = lax.broadcasted_iota(I32, (tq, bias_w), 1)
        bucket = _t5_bucket(c - r - bias_shift)
        for h in range(DSA_HEADS):
            bias = jnp.zeros((tq, bias_w), F32)
            for j in range(REL_BUCKETS):
                bias = jnp.where(bucket == j, rb_ref[j, h], bias)
            bias_s[h] = bias * (1.0 / math.log(2.0))

    iw = iw_ref[...]
    for h in range(IDX_HEADS):
        wb_s[h] = jnp.broadcast_to(iw[:, h:h + 1] * (IDX_DIM ** -0.5 * IDX_HEADS ** -0.5), (tq, LANE))

    rows = lax.broadcasted_iota(I32, (tq, tk), 0)
    cols = lax.broadcasted_iota(I32, (tq, tk), 1)
    q_chunk = (q0 + rows) >> CHUNK_SHIFT

    def grouped(n, body, carry):
        def group(g, c):
            for u in range(sup):
                c = body(g * sup + u, c)
            return c
        carry = lax.fori_loop(0, n // sup, group, carry)
        return lax.fori_loop(n // sup * sup, n, body, carry)

    def score_block(kb, carry):
        start = pl.multiple_of(kb * tk, tk)
        k_even = ka_s[pl.ds(start, tk), :]
        k_odd = kb_s[pl.ds(start, tk), :]
        halves = [jnp.zeros((tq, LANE), F32) for _ in range(tk // LANE)]
        for pair in range(IDX_HEADS // 2):
            lhs = iq_ref[:, pair * LANE:(pair + 1) * LANE]
            for kk, h in ((k_even, 2 * pair), (k_odd, 2 * pair + 1)):
                d = _dot_nt(lhs, kk)
                w = wb_s[h]
                for s in range(tk // LANE):
                    halves[s] = halves[s] + jnp.maximum(d[:, s * LANE:(s + 1) * LANE], 0.0) * w
        score = jnp.concatenate(halves, axis=1)
        score = jnp.where(score == 0.0, 0.0, score)
        admissible = ((start + cols) >> CHUNK_SHIFT) <= q_chunk
        score = jnp.where(admissible, score, -jnp.inf)
        bits = pltpu.bitcast(score, I32)
        keys_s[:, pl.ds(start, tk)] = jnp.where(bits < 0, bits ^ 0x7FFFFFFF, bits)
        return carry

    grouped(n_kb, score_block, 0)

    def pad_block(kb, carry):
        keys_s[:, pl.ds(pl.multiple_of(kb * tk, tk), tk)] = jnp.full((tq, tk), KEY_NEG_INF, I32)
        return carry

    lax.fori_loop(n_kb, (n_kb + sup - 1) // sup * sup, pad_block, 0)

    lane_pos = lax.broadcasted_iota(I32, (tq, LANE), 1)

    def count(pred):
        def body(kb, acc):
            for s in range(tk // LANE):
                first = pl.multiple_of(kb * tk + s * LANE, LANE)
                acc = acc + jnp.where(pred(keys_s[:, pl.ds(first, LANE)], first), 1.0, 0.0)
            return acc
        acc = grouped(n_kb, body, jnp.zeros((tq, LANE), F32))
        return jnp.broadcast_to(jnp.sum(acc, axis=1, keepdims=True), (tq, LANE))

    prefix_s[...] = jnp.full((tq, LANE), INT_MIN, I32)
    cnt_s[...] = jnp.full((tq, LANE), tk, F32) * n_kb.astype(F32)

    def select_cond(carry):
        it, settled = carry
        return jnp.logical_and(it < 32, settled == 0)

    def select_bit(carry):
        it, _ = carry
        bit = 31 - it
        prefix = prefix_s[...]
        cnt_old = cnt_s[...]
        settled = jnp.max(jnp.abs(cnt_old - topk)) == 0.0
        cand = jnp.where(bit == 31, jnp.zeros_like(prefix), prefix | jnp.left_shift(1, bit))
        cnt = count(lambda k, first: k >= cand)
        take = cnt >= topk
        prefix_s[...] = jnp.where(take, cand, prefix)
        cnt_s[...] = jnp.where(take, cnt, cnt_old)
        return it + 1, settled.astype(I32)

    lax.while_loop(select_cond, select_bit, (jnp.int32(0), jnp.int32(0)))
    kth = prefix_s[...]

    tied = jnp.where(kth > KEY_NEG_INF, cnt_s[...] - topk, 0.0)
    j_s[...] = jnp.full((tq, LANE), seq_len, I32)

    @pl.when(jnp.max(tied) > 0.0)
    def _break_ties():
        need = topk - count(lambda k, first: k > kth)

        def pos_bit(it, last):
            bit = (seq_len - 1).bit_length() - 1 - it
            cand = last | jnp.left_shift(1, bit)
            cnt = count(lambda k, first: jnp.where((first + lane_pos) < cand, k, INT_MIN) == kth)
            return jnp.where(cnt < need, cand, last)
        j_s[...] = lax.fori_loop(0, (seq_len - 1).bit_length(), pos_bit, jnp.zeros((tq, LANE), I32))

    last_tied = j_s[...]
    kth_eff = jnp.maximum(kth, KEY_NEG_INF + 1)

    log2e = 1.0 / math.log(2.0)
    m_s[...] = jnp.full_like(m_s, MASKED_LOGIT)
    l_s[...] = jnp.zeros_like(l_s)
    acc_s[...] = jnp.zeros_like(acc_s)
    n_sup = (n_kb + sup - 1) // sup

    def attend_super(sb, carry):
        base = pl.multiple_of(sb * (sup * tk), sup * tk)
        for t in range(sup * tk // LANE):
            first = pl.multiple_of(base + t * LANE, LANE)
            sel = keys_s[:, pl.ds(first, LANE)] >= jnp.where((first + lane_pos) > last_tied, kth_eff + 1, kth_eff)
            mask_s[:, t * LANE:(t + 1) * LANE] = jnp.where(sel, 0.0, -jnp.inf)
        def logits(h):
            hs = slice(h * HEAD_DIM, (h + 1) * HEAD_DIM)
            qh = qd_ref[:, hs]
            top = jnp.full((tq, LANE), -jnp.inf, F32)
            for j in range(sup):
                start = base + j * tk
                off = pl.multiple_of(jnp.clip(start - q0 + bias_shift, 0, bias_shift), LANE)
                s = (_dot_nt(qh, kd_ref[pl.ds(start, tk), hs]) * (scale * log2e)
                     + bias_s[h, :, pl.ds(off, tk)] + mask_s[:, j * tk:(j + 1) * tk])
                logit_s[h, :, j * tk:(j + 1) * tk] = s
                for t in range(tk // LANE):
                    top = jnp.maximum(top, s[:, t * LANE:(t + 1) * LANE])
            m_old = m_s[h]
            m_new = jnp.maximum(m_old, jnp.max(top, axis=1, keepdims=True))
            m_s[h] = m_new
            return m_new, jnp.exp2(m_old - m_new)

        def weigh(h, m_new, alpha):
            hs = slice(h * HEAD_DIM, (h + 1) * HEAD_DIM)
            pv = jnp.zeros((tq, HEAD_DIM), F32)
            psum = jnp.zeros((tq, LANE), F32)
            for j in range(sup):
                start = base + j * tk
                tiles = [jnp.exp2(logit_s[h, :, j * tk + t * LANE:j * tk + (t + 1) * LANE] - m_new)
                         for t in range(tk // LANE)]
                for p_t in tiles:
                    psum = psum + p_t
                p = jnp.concatenate([p_t.astype(BF16) for p_t in tiles], axis=1)
                pv = pv + _dot(p, vd_ref[pl.ds(start, tk), hs])
            acc_s[h] = alpha * acc_s[h] + pv
            l_s[h] = alpha * l_s[h] + psum

        nxt = logits(0)
        for h in range(DSA_HEADS):
            cur = nxt
            if h + 1 < DSA_HEADS:
                nxt = logits(h + 1)
            weigh(h, *cur)
        return carry

    lax.fori_loop(0, n_sup, attend_super, 0)
    for h in range(DSA_HEADS):
        denom = jnp.sum(l_s[h], axis=1, keepdims=True)
        o_ref[:, h * HEAD_DIM:(h + 1) * HEAD_DIM] = (acc_s[h] / denom).astype(o_ref.dtype)


def _dsa_attention(proj, side, rel_bias, batch, seq_len, cols):
    n = proj.shape[0]
    tq = min(DSA_TQ, seq_len)
    tk = min(DSA_TK, seq_len)
    nq = seq_len // tq
    topk = min(DSA_MAX_TOPK, seq_len // 4)
    bias_w = tk + 2 * tk
    sup = min(DSA_SUPER, seq_len // tk)
    kern = functools.partial(_dsa_kernel, tq=tq, tk=tk, sup=sup, topk=topk, seq_len=seq_len,
                             scale=HEAD_DIM ** -0.5)
    once = pl.Buffered(1)
    return pl.pallas_call(
        kern,
        grid=(batch, nq),
        in_specs=[
            pl.BlockSpec(memory_space=pltpu.SMEM),
            pl.BlockSpec((tq, DSA_WIDTH), lambda b, i: (b * nq + i, cols["dq"])),
            pl.BlockSpec((seq_len, DSA_WIDTH), lambda b, i: (b, cols["dk"]), pipeline_mode=once),
            pl.BlockSpec((seq_len, DSA_WIDTH), lambda b, i: (b, cols["dv"]), pipeline_mode=once),
            pl.BlockSpec((tq, IDX_HEADS * IDX_DIM), lambda b, i: (b * nq + i, cols["iq"])),
            pl.BlockSpec((seq_len, LANE), lambda b, i: (b, cols["ik_even"]), pipeline_mode=once),
            pl.BlockSpec((seq_len, LANE), lambda b, i: (b, cols["ik_odd"]), pipeline_mode=once),
            pl.BlockSpec((tq, LANE), lambda b, i: (b * nq + i, cols["iw"])),
        ],
        out_specs=pl.BlockSpec((tq, DSA_WIDTH), lambda b, i: (b * nq + i, 0)),
        out_shape=jax.ShapeDtypeStruct((n, DSA_WIDTH), BF16),
        scratch_shapes=[
            pltpu.VMEM((seq_len, LANE), BF16),
            pltpu.VMEM((seq_len, LANE), BF16),
            pltpu.VMEM((IDX_HEADS, tq, LANE), F32),
            pltpu.VMEM((tq, seq_len), I32),
            pltpu.VMEM((DSA_HEADS, tq, bias_w), F32),
            pltpu.VMEM((DSA_HEADS, tq, HEAD_DIM), F32),
            pltpu.VMEM((DSA_HEADS, tq, LANE), F32),
            pltpu.VMEM((DSA_HEADS, tq, LANE), F32),
            pltpu.VMEM((tq, LANE), I32),
            pltpu.VMEM((tq, LANE), I32),
            pltpu.VMEM((tq, LANE), F32),
            pltpu.VMEM((tq, sup * tk), F32),
            pltpu.VMEM((DSA_HEADS, tq, sup * tk), F32),
        ],
        compiler_params=_params(2),
        name="dsa_attention",
    )(rel_bias, proj, proj, proj, proj, side, side, side)


def _s5_operators(a_re, a_im, log_dt, b_re, b_im, c_re, c_im):
    hp = lax.Precision.HIGHEST
    L = SSM_CHUNK
    ar, ai = a_re.astype(F32), a_im.astype(F32)
    dt = jnp.exp(log_dt.astype(F32))[:, None]
    mag = jnp.exp(dt * ar)
    abar_re, abar_im = mag * jnp.cos(dt * ai), mag * jnp.sin(dt * ai)
    den = ar * ar + ai * ai
    nr = abar_re - 1.0
    f_re = (nr * ar + abar_im * ai) / den
    f_im = (abar_im * ar - nr * ai) / den
    br, bi = b_re.astype(F32), b_im.astype(F32)
    bb_re = f_re[..., None] * br - f_im[..., None] * bi
    bb_im = f_re[..., None] * bi + f_im[..., None] * br
    pw_re, pw_im = [jnp.ones_like(abar_re)], [jnp.zeros_like(abar_im)]
    for _ in range(L):
        pr, pi = pw_re[-1], pw_im[-1]
        pw_re.append(pr * abar_re - pi * abar_im)
        pw_im.append(pr * abar_im + pi * abar_re)
    pw_re, pw_im = jnp.stack(pw_re, 1), jnp.stack(pw_im, 1)
    cr, ci = c_re.astype(F32), c_im.astype(F32)
    cp_re = cr[:, None] * pw_re[:, :, None, :] - ci[:, None] * pw_im[:, :, None, :]
    cp_im = cr[:, None] * pw_im[:, :, None, :] + ci[:, None] * pw_re[:, :, None, :]
    kern = (jnp.einsum('gtop,gpi->gtoi', cp_re[:, :L], bb_re, precision=hp)
            - jnp.einsum('gtop,gpi->gtoi', cp_im[:, :L], bb_im, precision=hp))
    s_idx = jnp.arange(L)[:, None]
    t_idx = jnp.arange(L)[None, :]
    lag = t_idx - s_idx
    toep = jnp.where((lag >= 0)[None, :, :, None, None], kern[:, jnp.clip(lag, 0, L - 1)], 0.0)
    G = ar.shape[0]
    hc = SSM_GROUP
    mt = jnp.transpose(toep, (0, 1, 4, 2, 3)).reshape(G, L * hc, L * hc)
    rev_re, rev_im = pw_re[:, L - 1::-1][:, :L], pw_im[:, L - 1::-1][:, :L]
    w_re = (rev_re[:, :, None, :] * jnp.swapaxes(bb_re, 1, 2)[:, None]
            - rev_im[:, :, None, :] * jnp.swapaxes(bb_im, 1, 2)[:, None]).reshape(G, L * hc, SSM_STATE)
    w_im = (rev_re[:, :, None, :] * jnp.swapaxes(bb_im, 1, 2)[:, None]
            + rev_im[:, :, None, :] * jnp.swapaxes(bb_re, 1, 2)[:, None]).reshape(G, L * hc, SSM_STATE)
    v_re = jnp.transpose(cp_re[:, 1:], (0, 3, 1, 2)).reshape(G, SSM_STATE, L * hc)
    v_im = -jnp.transpose(cp_im[:, 1:], (0, 3, 1, 2)).reshape(G, SSM_STATE, L * hc)
    lam = jnp.stack([pw_re[:, L], pw_im[:, L]], axis=1)
    return mt, w_re, w_im, v_re, v_im, lam


def _to_chunks_kernel(*refs, hc):
    *u_refs, o_ref = refs
    nc = o_ref.shape[1]
    per_tile = LANE // hc
    for j, u_ref in enumerate(u_refs):
        steps = [u_ref[pl.ds(s, nc, stride=SSM_CHUNK), :].T for s in range(SSM_CHUNK)]
        for gl in range(per_tile):
            o_ref[j * per_tile + gl] = jnp.concatenate(
                [st[gl * hc:(gl + 1) * hc, :] for st in steps], axis=0).T


def _from_chunks_kernel(y_ref, o_ref, *tile_s, hc):
    nc = y_ref.shape[1]
    per_tile = LANE // hc
    for j, t_s in enumerate(tile_s):
        per_group = [y_ref[j * per_tile + gl].T for gl in range(per_tile)]
        for s in range(SSM_CHUNK):
            t_s[pl.ds(s, nc, stride=SSM_CHUNK), :] = jnp.concatenate(
                [yg[s * hc:(s + 1) * hc, :] for yg in per_group], axis=0).T
        o_ref[:, j * LANE:(j + 1) * LANE] = t_s[...]


def _s5_to_chunks(side, width, groups, rows):
    n = side.shape[0]
    nc = rows // SSM_CHUNK
    hc = width // groups
    tiles = width // LANE
    return pl.pallas_call(
        functools.partial(_to_chunks_kernel, hc=hc),
        grid=(n // rows,),
        in_specs=[pl.BlockSpec((rows, LANE), lambda i, j=j: (i, j)) for j in range(tiles)],
        out_specs=pl.BlockSpec((groups, nc, SSM_CHUNK * hc), lambda i: (0, i, 0)),
        out_shape=jax.ShapeDtypeStruct((groups, n // SSM_CHUNK, SSM_CHUNK * hc), F32),
        compiler_params=_params(1),
        name="s5_to_chunks",
    )(*([side] * tiles))


def _s5_from_chunks(y_chunks, rows):
    groups, n_chunks, cw = y_chunks.shape
    hc = cw // SSM_CHUNK
    width = groups * hc
    nc = rows // SSM_CHUNK
    return pl.pallas_call(
        functools.partial(_from_chunks_kernel, hc=hc),
        grid=(n_chunks // nc,),
        in_specs=[pl.BlockSpec((groups, nc, cw), lambda i: (0, i, 0))],
        out_specs=pl.BlockSpec((rows, width), lambda i: (i, 0)),
        out_shape=jax.ShapeDtypeStruct((n_chunks * SSM_CHUNK, width), F32),
        scratch_shapes=[pltpu.VMEM((rows, LANE), F32)] * (width // LANE),
        compiler_params=_params(1),
        name="s5_from_chunks",
    )(y_chunks)


def _s5_kernel(u_ref, mt_ref, wre_ref, wim_ref, vre_ref, vim_ref, lam_ref, y_ref,
               sre_s, sim_s, hre_s, him_s, *, batch):
    u = u_ref[...]
    sre_s[...] = _dot3(u, wre_ref[...])
    sim_s[...] = _dot3(u, wim_ref[...])
    lre = lam_ref[0:1, :]
    lim = lam_ref[1:2, :]
    n_chunks = u.shape[0] // batch
    state_w = lre.shape[1]

    def step(c, carry):
        out = []
        for b in range(batch):
            hr, hi = carry[b]
            row = b * n_chunks + c
            hre_s[pl.ds(row, 1), :] = hr
            him_s[pl.ds(row, 1), :] = hi
            sr = sre_s[pl.ds(row, 1), :]
            si = sim_s[pl.ds(row, 1), :]
            out.append((lre * hr - lim * hi + sr, lre * hi + lim * hr + si))
        return tuple(out)

    zero = jnp.zeros((1, state_w), F32)
    lax.fori_loop(0, n_chunks, step, tuple((zero, zero) for _ in range(batch)))
    y_ref[...] = (_dot3(u, mt_ref[...]) + _dot3(hre_s[...], vre_ref[...])
                  + _dot3(him_s[...], vim_ref[...]))


def _s5_scan(u_chunks, ops, batch):
    mt, w_re, w_im, v_re, v_im, lam = ops
    G, nc, cw = u_chunks.shape
    P = SSM_STATE
    grp = lambda *shape: pl.BlockSpec((None,) + shape, lambda g: (g,) + (0,) * len(shape))
    return pl.pallas_call(
        functools.partial(_s5_kernel, batch=batch),
        grid=(G,),
        in_specs=[grp(nc, cw), grp(cw, cw), grp(cw, P), grp(cw, P), grp(P, cw), grp(P, cw), grp(2, P)],
        out_specs=grp(nc, cw),
        out_shape=jax.ShapeDtypeStruct((G, nc, cw), F32),
        scratch_shapes=[pltpu.VMEM((nc, P), F32)] * 4,
        compiler_params=_params(1),
        name="s5_scan",
    )(u_chunks, mt, w_re, w_im, v_re, v_im, lam)


def _glu_kernel(y_ref, u_ref, d_ref, w_ref, b_ref, o_ref):
    y = y_ref[...] + d_ref[...] * u_ref[...]
    y = 0.5 * y * (1.0 + jnp.tanh(math.sqrt(2.0 / math.pi) * (y + 0.044715 * (y * y * y))))
    gate = jax.nn.sigmoid(_dot(y.astype(BF16), w_ref[...]) + b_ref[...])
    o_ref[...] = (y * gate).astype(o_ref.dtype)


def _s5_glu(y_scan, side, d_skip, w_glu, b_glu, tm):
    n, width = y_scan.shape
    row = lambda: pl.BlockSpec((1, width), lambda i: (0, 0))
    return pl.pallas_call(
        _glu_kernel,
        grid=(n // tm,),
        in_specs=[
            pl.BlockSpec((tm, width), lambda i: (i, 0)),
            pl.BlockSpec((tm, width), lambda i: (i, 0)),
            row(),
            pl.BlockSpec((width, width), lambda i: (0, 0)),
            row(),
        ],
        out_specs=pl.BlockSpec((tm, width), lambda i: (i, 0)),
        out_shape=jax.ShapeDtypeStruct((n, width), BF16),
        compiler_params=_params(1),
        name="s5_glu",
    )(y_scan, side, d_skip.reshape(1, width), w_glu.astype(BF16), b_glu.reshape(1, width))


def _outproj_kernel(sb_ref, ssm_ref, dsa_ref, w_ref, x_ref, gate_ref, g_ref, b_ref, sc_ref, sh_ref,
                    x_out_ref, u_out_ref, *, alpha):
    k0 = sb_ref.shape[1]
    k1 = k0 + ssm_ref.shape[1]
    mix = (_dot(sb_ref[...], w_ref[0:k0, :]) + _dot(ssm_ref[...], w_ref[k0:k1, :])
           + _dot(dsa_ref[...], w_ref[k1:, :]))
    y = alpha * x_ref[...] + (1.0 + gate_ref[...]) * mix
    x1 = _layer_norm(y, g_ref[...], b_ref[...])
    x_out_ref[...] = x1
    u_out_ref[...] = (x1 * (1.0 + sc_ref[...]) + sh_ref[...]).astype(u_out_ref.dtype)


def _outproj_ln(sb_o, ssm_o, dsa_o, w_out, layer, x, gate, ln_g, ln_b, scale, shift, seq_len, alpha, tm):
    n, d = x.shape
    per_batch = lambda: pl.BlockSpec((None, 1, d), lambda i: ((i * tm) // seq_len, 0, 0))
    vec = lambda: pl.BlockSpec((1, d), lambda i: (0, 0))
    tile = lambda w: pl.BlockSpec((tm, w), lambda i: (i, 0))
    return pl.pallas_call(
        functools.partial(_outproj_kernel, alpha=alpha),
        grid=(n // tm,),
        in_specs=[tile(sb_o.shape[1]), tile(ssm_o.shape[1]), tile(dsa_o.shape[1]),
                  pl.BlockSpec((None,) + w_out.shape[1:], lambda i: (layer, 0, 0)),
                  tile(d), per_batch(), vec(), vec(), per_batch(), per_batch()],
        out_specs=[tile(d), tile(d)],
        out_shape=[jax.ShapeDtypeStruct((n, d), F32), jax.ShapeDtypeStruct((n, d), BF16)],
        compiler_params=_params(1),
        name="outproj_ln",
    )(sb_o, ssm_o, dsa_o, w_out, x, gate, ln_g.reshape(1, d), ln_b.reshape(1, d), scale, shift)


def _ffn_up_kernel(u_ref, halo_ref, wg_ref, wv_ref, cwg_ref, cwv_ref, cbg_ref, cbv_ref, o_ref,
                   *, tm, halo, seq_len, sub):
    i = pl.program_id(0)
    has_prev = jnp.where((i * tm) % seq_len == 0, 0.0, 1.0)
    ux = jnp.concatenate([halo_ref[...] * has_prev.astype(BF16), u_ref[...]], axis=0)

    def conv(w_ref, cw_ref, cb_ref, cs):
        a = _dot(ux, w_ref[:, cs])
        out = cb_ref[:, cs] + cw_ref[CONV_WIDTH - 1:CONV_WIDTH, cs] * a[halo:, :]
        for back in range(1, CONV_WIDTH):
            tap = CONV_WIDTH - 1 - back
            out = out + cw_ref[tap:tap + 1, cs] * pltpu.roll(a, back, 0)[halo:, :]
        return out

    tf = o_ref.shape[1]
    for c in range(pl.cdiv(tf, sub)):
        cs = slice(c * sub, min((c + 1) * sub, tf))
        g = conv(wg_ref, cwg_ref, cbg_ref, cs)
        val = conv(wv_ref.at[0], cwv_ref, cbv_ref, cs)
        o_ref[:, cs] = (g * jax.nn.sigmoid(g) * val).astype(o_ref.dtype)


def _ffn_up(u, w_up, conv_w, conv_b, layer, d_ff, seq_len, tm, tf):
    n, d = u.shape
    ffp = conv_w.shape[2] // 2
    assert w_up.shape[2] == d_ff + ffp and d_ff % LANE == 0
    nj = ffp // tf
    halo = 16
    sub = 2 * LANE
    return pl.pallas_call(
        functools.partial(_ffn_up_kernel, tm=tm, halo=halo, seq_len=seq_len, sub=sub),
        grid=(n // tm, nj),
        in_specs=[
            pl.BlockSpec((tm, d), lambda i, j: (i, 0)),
            pl.BlockSpec((halo, d), lambda i, j: (jnp.maximum(i * (tm // halo) - 1, 0), 0)),
            pl.BlockSpec((None, d, tf), lambda i, j: (layer, 0, j)),
            pl.BlockSpec((pl.Element(1), pl.Element(d), pl.Element(tf)),
                         lambda i, j: (layer, 0, (j * (tf // LANE) + d_ff // LANE) * LANE)),
            pl.BlockSpec((None, CONV_WIDTH, tf), lambda i, j: (layer, 0, j)),
            pl.BlockSpec((None, CONV_WIDTH, tf), lambda i, j: (layer, 0, j + nj)),
            pl.BlockSpec((None, 1, tf), lambda i, j: (layer, 0, j)),
            pl.BlockSpec((None, 1, tf), lambda i, j: (layer, 0, j + nj)),
        ],
        out_specs=pl.BlockSpec((tm, tf), lambda i, j: (i, j)),
        out_shape=jax.ShapeDtypeStruct((n, ffp), BF16),
        compiler_params=_params(2),
        name="ffn_up_conv_gate",
    )(u, u, w_up, w_up, conv_w, conv_w, conv_b, conv_b)


def _ffn_down_kernel(h_ref, w_ref, x_ref, gate_ref, g_ref, b_ref, o_ref, *, alpha):
    y = alpha * x_ref[...] + (1.0 + gate_ref[...]) * _dot(h_ref[...], w_ref[...])
    o_ref[...] = _layer_norm(y, g_ref[...], b_ref[...])


def _ffn_down_ln(h, w_down, layer, x, gate, ln_g, ln_b, seq_len, alpha, tm):
    n, d = x.shape
    ffp = h.shape[1]
    return pl.pallas_call(
        functools.partial(_ffn_down_kernel, alpha=alpha),
        grid=(n // tm,),
        in_specs=[
            pl.BlockSpec((tm, ffp), lambda i: (i, 0)),
            pl.BlockSpec((None, ffp, d), lambda i: (layer, 0, 0), pipeline_mode=pl.Buffered(1)),
            pl.BlockSpec((tm, d), lambda i: (i, 0)),
            pl.BlockSpec((None, 1, d), lambda i: ((i * tm) // seq_len, 0, 0)),
            pl.BlockSpec((1, d), lambda i: (0, 0)),
            pl.BlockSpec((1, d), lambda i: (0, 0)),
        ],
        out_specs=pl.BlockSpec((tm, d), lambda i: (i, 0)),
        out_shape=jax.ShapeDtypeStruct((n, d), F32),
        compiler_params=_params(1),
        name="ffn_down_ln",
    )(h, w_down, x, gate, ln_g.reshape(1, d), ln_b.reshape(1, d))


def _round_up(v, m):
    return (v + m - 1) // m * m


def _split_w_in(w_in):
    w = w_in.astype(BF16)
    sizes = (SB_WIDTH, SB_WIDTH, SB_WIDTH, w.shape[2] - 3 * SB_WIDTH - 3 * DSA_WIDTH
             - IDX_HEADS * IDX_DIM - IDX_DIM - IDX_HEADS, DSA_WIDTH, DSA_WIDTH, DSA_WIDTH,
             IDX_HEADS * IDX_DIM, IDX_DIM, IDX_HEADS)
    offs = [0]
    for s in sizes:
        offs.append(offs[-1] + s)
    sbq, sbk, sbv, ssm, dq, dk, dv, iq, ik, iw = (w[:, :, offs[t]:offs[t + 1]] for t in range(10))
    z = lambda width: jnp.zeros(w.shape[:2] + (width,), BF16)
    main = jnp.concatenate([dq, dk, dv, sbq, sbk, sbv, ssm, iq], axis=2)
    side = jnp.concatenate([ssm, ik, z(LANE - IDX_DIM), z(LANE - IDX_DIM), ik, iw, z(LANE - IDX_HEADS)],
                           axis=2)
    return main, side, sizes[3]


def _pad_halves(a, d_ff, ffp):
    lead = a.shape[:-1]
    halves = a.reshape(lead + (2, d_ff))
    halves = jnp.pad(halves, [(0, 0)] * (len(lead) + 1) + [(0, ffp - d_ff)])
    return halves.reshape(lead + (2 * ffp,))


def kernel(x, c, w_ada, b_ada, w_in, w_out, ssm_a_re, ssm_a_im, ssm_log_dt, ssm_b_re, ssm_b_im,
           ssm_c_re, ssm_c_im, ssm_d, w_glu, b_glu, rel_bias, ln1_g, ln1_b, w_up, conv_w, conv_b,
           w_down, ln2_g, ln2_b):
    batch, seq_len, d_model = x.shape
    depth = w_ada.shape[0]
    n = batch * seq_len
    alpha = (2 * depth) ** 0.25
    d_ff = w_down.shape[1]
    ffp = _round_up(d_ff, 512)
    tm_proj = min(1024, seq_len)
    tm = min(512, seq_len)

    mod = _adaln_mod(c, w_ada, b_ada)
    w_main, w_side, ssm_width = _split_w_in(w_in)
    w_out_b = w_out.astype(BF16)
    w_up_p = jnp.pad(w_up.astype(BF16), ((0, 0), (0, 0), (0, ffp - d_ff)))
    conv_w_p = _pad_halves(conv_w, d_ff, ffp)
    conv_b_p = _pad_halves(conv_b[:, None, :], d_ff, ffp)
    w_down_p = jnp.pad(w_down.astype(BF16), ((0, 0), (0, ffp - d_ff), (0, 0)))
    relayout_rows = min(2048, seq_len)

    xf = x.reshape(n, d_model)
    for l in range(depth):
        sh_m, sc_m, g_m, sh_f, sc_f, g_f = (
            mod[l, :batch, t * d_model:(t + 1) * d_model].reshape(batch, 1, d_model) for t in range(N_MOD))
        proj = _modulated_matmul(xf, sc_m, sh_m, w_main, l, BF16, seq_len, tm_proj, 512, "in_proj")
        side = _modulated_matmul(xf, sc_m, sh_m, w_side, l, F32, seq_len, tm_proj, w_side.shape[2],
                                 "in_proj_side")

        sb_o = _sb_attention(proj, batch, seq_len, 3 * DSA_HEADS, 3 * DSA_HEADS + SB_HEADS,
                             3 * DSA_HEADS + 2 * SB_HEADS)
        side_blk = ssm_width // LANE
        dsa_cols = {"dq": 0, "dk": 1, "dv": 2,
                    "iq": (3 * DSA_WIDTH + 3 * SB_WIDTH + ssm_width) // (IDX_HEADS * IDX_DIM),
                    "ik_even": side_blk, "ik_odd": side_blk + 1, "iw": side_blk + 2}
        dsa_o = _dsa_attention(proj, side, rel_bias, batch, seq_len, dsa_cols)

        groups = ssm_width // SSM_GROUP
        u_chunks = _s5_to_chunks(side, ssm_width, groups, relayout_rows)
        ops = _s5_operators(ssm_a_re[l], ssm_a_im[l], ssm_log_dt[l], ssm_b_re[l], ssm_b_im[l],
                            ssm_c_re[l], ssm_c_im[l])
        y_chunks = _s5_scan(u_chunks, ops, batch)
        y_scan = _s5_from_chunks(y_chunks, relayout_rows)
        ssm_o = _s5_glu(y_scan, side, ssm_d[l], w_glu[l], b_glu[l], tm)

        x1, u_f = _outproj_ln(sb_o, ssm_o, dsa_o, w_out_b, l, xf, g_m, ln1_g[l], ln1_b[l],
                              sc_f, sh_f, seq_len, alpha, tm)
        h = _ffn_up(u_f, w_up_p, conv_w_p, conv_b_p, l, d_ff, seq_len, tm, 512)
        xf = _ffn_down_ln(h, w_down_p, l, x1, g_f, ln2_g[l], ln2_b[l], seq_len, alpha, tm)
    return xf.reshape(batch, seq_len, d_model)
```

```python
import functools
import math

import jax
import jax.numpy as jnp
from jax import lax
from jax.experimental import pallas as pl
from jax.experimental.pallas import tpu as pltpu

F32 = jnp.float32
BF16 = jnp.bfloat16
I32 = jnp.int32

CHUNK = 64
CHUNK_SHIFT = CHUNK.bit_length() - 1
assert 1 << CHUNK_SHIFT == CHUNK
SB_HEADS = 6
DSA_HEADS = 6
HEAD_DIM = 128
SB_WIDTH = SB_HEADS * HEAD_DIM
DSA_WIDTH = DSA_HEADS * HEAD_DIM
SSM_GROUP = 16
SSM_STATE = 64
IDX_HEADS = 16
IDX_DIM = 64
DSA_MAX_TOPK = 256
REL_BUCKETS = 32
REL_MAX_DIST = 128
CONV_WIDTH = 3
N_MOD = 6
LN_EPS = 1e-5

LANE = 128
VMEM_LIMIT_BYTES = 56 * 1024 * 1024

SSM_CHUNK = 16
SB_BLOCK = 256
SB_HEAD_GROUP = 3
SB_ZERO_LOG = 104.0
DSA_TQ = 128
DSA_TK = 256
DSA_SUPER = 4
KEY_NEG_INF = -2139095041
INT_MIN = -2147483648
MASKED_LOGIT = -1e30


def _params(n_axes):
    return pltpu.CompilerParams(dimension_semantics=("arbitrary",) * n_axes,
                                vmem_limit_bytes=VMEM_LIMIT_BYTES)


def _dot(a, b):
    return jnp.dot(a, b, preferred_element_type=F32)


def _dot_nt(a, b):
    return lax.dot_general(a, b, (((1,), (1,)), ((), ())), preferred_element_type=F32)


def _split_bf16(a):
    hi = a.astype(BF16)
    lo = (a - hi.astype(F32)).astype(BF16)
    return hi, lo


def _dot3(a, b):
    ah, al = _split_bf16(a)
    bh, bl = _split_bf16(b)
    return _dot(ah, bh) + _dot(ah, bl) + _dot(al, bh)


def _layer_norm(y, g, b):
    mu = jnp.mean(y, axis=-1, keepdims=True)
    d = y - mu
    var = jnp.mean(d * d, axis=-1, keepdims=True)
    return d * lax.rsqrt(var + LN_EPS) * g + b


def _mod_kernel(c_ref, w_ref, b_ref, o_ref):
    c = c_ref[...]
    cond = c * jax.nn.sigmoid(c)
    o_ref[...] = _dot(cond.astype(BF16), w_ref[...].astype(BF16)) + b_ref[...]


def _adaln_mod(c, w_ada, b_ada):
    depth, d_model, n_out = w_ada.shape
    rows = 8
    c_pad = jnp.zeros((rows, d_model), F32).at[: c.shape[0]].set(c)
    tn = 1024
    return pl.pallas_call(
        _mod_kernel,
        grid=(depth, n_out // tn),
        in_specs=[
            pl.BlockSpec((rows, d_model), lambda l, j: (0, 0)),
            pl.BlockSpec((None, d_model, tn), lambda l, j: (l, 0, j)),
            pl.BlockSpec((None, 1, tn), lambda l, j: (l, 0, j)),
        ],
        out_specs=pl.BlockSpec((None, rows, tn), lambda l, j: (l, 0, j)),
        out_shape=jax.ShapeDtypeStruct((depth, rows, n_out), F32),
        compiler_params=_params(2),
        name="adaln_mod",
    )(c_pad, w_ada, b_ada.reshape(depth, 1, n_out))


def _inproj_kernel(x_ref, sc_ref, sh_ref, w_ref, o_ref, u_ref):
    @pl.when(pl.program_id(1) == 0)
    def _():
        u_ref[...] = (x_ref[...] * (1.0 + sc_ref[...]) + sh_ref[...]).astype(BF16)

    o_ref[...] = _dot(u_ref[...], w_ref[...]).astype(o_ref.dtype)


def _modulated_matmul(x, scale, shift, w, layer, out_dtype, seq_len, tm, tn, name):
    n, d = x.shape
    ncols = w.shape[2]
    return pl.pallas_call(
        _inproj_kernel,
        grid=(n // tm, ncols // tn),
        in_specs=[
            pl.BlockSpec((tm, d), lambda i, j: (i, 0)),
            pl.BlockSpec((None, 1, d), lambda i, j: ((i * tm) // seq_len, 0, 0)),
            pl.BlockSpec((None, 1, d), lambda i, j: ((i * tm) // seq_len, 0, 0)),
            pl.BlockSpec((None, d, tn), lambda i, j: (layer, 0, j)),
        ],
        out_specs=pl.BlockSpec((tm, tn), lambda i, j: (i, j)),
        out_shape=jax.ShapeDtypeStruct((n, ncols), out_dtype),
        scratch_shapes=[pltpu.VMEM((tm, d), BF16)],
        compiler_params=_params(2),
        name=name,
    )(x, scale, shift, w)


def _sb_kernel(q_ref, k_ref, v_ref, o_ref, acc_ref, run_ref, *, blk, scale, heads):
    i = pl.program_id(2)
    rows = lax.broadcasted_iota(I32, (blk, blk), 0)
    cols = lax.broadcasted_iota(I32, (blk, blk), 1)
    tri = jnp.where(rows > cols, 1.0, 0.0).astype(BF16)
    keep = cols < rows

    def block(kb, masked):
        start = pl.multiple_of(kb * blk, blk)
        head_cols = [slice(h * HEAD_DIM, (h + 1) * HEAD_DIM) for h in range(heads)]
        log_sig, log_1m, split = [], [], []
        for hs in head_cols:
            z = _dot_nt(q_ref[:, hs], k_ref[pl.ds(start, blk), hs]) * scale
            softplus = jnp.maximum(z, 0.0) + jnp.log(1.0 + jnp.exp(-jnp.abs(z)))
            l1m = -softplus
            if masked:
                l1m = jnp.where(keep, l1m, 0.0)
            log_sig.append(z - softplus)
            log_1m.append(l1m)
            split.append(_split_bf16(l1m))
        suffix = [_dot(hi, tri) + _dot(lo, tri) for hi, lo in split]
        tops = []
        for h, hs in enumerate(head_cols):
            run = run_ref[h]
            w = jnp.exp(log_sig[h] + suffix[h] + run)
            if masked:
                w = jnp.where(keep, w, 0.0)
            acc_ref[h] += _dot(w.astype(BF16), v_ref[pl.ds(start, blk), hs])
            run_new = run + jnp.sum(log_1m[h], axis=1, keepdims=True)
            run_ref[h] = run_new
            tops.append(jnp.max(run_new))
        return functools.reduce(jnp.maximum, tops)

    acc_ref[...] = jnp.zeros_like(acc_ref)
    run_ref[...] = jnp.zeros_like(run_ref)
    top = block(i, True)

    def cond(carry):
        kb, top = carry
        return jnp.logical_and(kb >= 0, top > -SB_ZERO_LOG)

    def body(carry):
        kb, _ = carry
        return kb - 1, block(kb, False)

    lax.while_loop(cond, body, (i - 1, top))
    for h in range(heads):
        o_ref[:, h * HEAD_DIM:(h + 1) * HEAD_DIM] = acc_ref[h].astype(o_ref.dtype)


def _sb_attention(proj, batch, seq_len, q_col, k_col, v_col):
    n = proj.shape[0]
    blk = min(SB_BLOCK, seq_len)
    nq = seq_len // blk
    hg = SB_HEAD_GROUP
    assert SB_HEADS % hg == 0 and q_col % hg == 0 and k_col % hg == 0 and v_col % hg == 0
    width = hg * HEAD_DIM
    kern = functools.partial(_sb_kernel, blk=blk, scale=HEAD_DIM ** -0.5, heads=hg)
    return pl.pallas_call(
        kern,
        grid=(batch, SB_HEADS // hg, nq),
        in_specs=[
            pl.BlockSpec((blk, width), lambda b, g, i: (b * nq + i, q_col // hg + g)),
            pl.BlockSpec((seq_len, width), lambda b, g, i: (b, k_col // hg + g)),
            pl.BlockSpec((seq_len, width), lambda b, g, i: (b, v_col // hg + g)),
        ],
        out_specs=pl.BlockSpec((blk, width), lambda b, g, i: (b * nq + i, g)),
        out_shape=jax.ShapeDtypeStruct((n, SB_WIDTH), BF16),
        scratch_shapes=[pltpu.VMEM((hg, blk, HEAD_DIM), F32), pltpu.VMEM((hg, blk, 1), F32)],
        compiler_params=_params(3),
        name="sb_attention",
    )(proj, proj, proj)


def _t5_bucket(rel):
    half = REL_BUCKETS // 2
    max_exact = half // 2
    n = jnp.abs(rel)
    nf = jnp.maximum(n, 1).astype(F32)
    large = max_exact + (jnp.log(nf / max_exact) / math.log(REL_MAX_DIST / max_exact)
                         * (half - max_exact)).astype(I32)
    large = jnp.minimum(large, half - 1)
    return jnp.where(rel > 0, half, 0) + jnp.where(n < max_exact, n, large)


def _dsa_kernel(rb_ref, qd_ref, kd_ref, vd_ref, iq_ref, ika_ref, ikb_ref, iw_ref, o_ref,
                ka_s, kb_s, wb_s, keys_s, bias_s, acc_s, m_s, l_s, j_s, prefix_s, cnt_s, mask_s, logit_s,
                *, tq, tk, sup, topk, seq_len, scale):
    i = pl.program_id(1)
    q0 = i * tq
    n_kb = (q0 + tq + tk - 1) // tk
    bias_w = bias_s.shape[2]
    bias_shift = bias_w - tk

    @pl.when(i == 0)
    def _setup():
        ka_s[...] = ika_ref[...].astype(BF16)
        kb_s[...] = ikb_ref[...].astype(BF16)
        r = lax.broadcasted_iota(I32, (tq, bias_w), 0)
        c = lax.broadcasted_iota(I32, (tq, bias_w), 1)
        bucket = _t5_bucket(c - r - bias_shift)
        for h in range(DSA_HEADS):
            bias = jnp.zeros((tq, bias_w), F32)
            for j in range(REL_BUCKETS):
                bias = jnp.where(bucket == j, rb_ref[j, h], bias)
            bias_s[h] = bias * (1.0 / math.log(2.0))

    iw = iw_ref[...]
    for h in range(IDX_HEADS):
        wb_s[h] = jnp.broadcast_to(iw[:, h:h + 1] * (IDX_DIM ** -0.5 * IDX_HEADS ** -0.5), (tq, LANE))

    rows = lax.broadcasted_iota(I32, (tq, tk), 0)
    cols = lax.broadcasted_iota(I32, (tq, tk), 1)
    q_chunk = (q0 + rows) >> CHUNK_SHIFT

    def grouped(n, body, carry):
        def group(g, c):
            for u in range(sup):
                c = body(g * sup + u, c)
            return c
        carry = lax.fori_loop(0, n // sup, group, carry)
        return lax.fori_loop(n // sup * sup, n, body, carry)

    def score_block(kb, carry):
        start = pl.multiple_of(kb * tk, tk)
        k_even = ka_s[pl.ds(start, tk), :]
        k_odd = kb_s[pl.ds(start, tk), :]
        halves = [jnp.zeros((tq, LANE), F32) for _ in range(tk // LANE)]
        for pair in range(IDX_HEADS // 2):
            lhs = iq_ref[:, pair * LANE:(pair + 1) * LANE]
            for kk, h in ((k_even, 2 * pair), (k_odd, 2 * pair + 1)):
                d = _dot_nt(lhs, kk)
                w = wb_s[h]
                for s in range(tk // LANE):
                    halves[s] = halves[s] + jnp.maximum(d[:, s * LANE:(s + 1) * LANE], 0.0) * w
        score = jnp.concatenate(halves, axis=1)
        score = jnp.where(score == 0.0, 0.0, score)
        admissible = ((start + cols) >> CHUNK_SHIFT) <= q_chunk
        score = jnp.where(admissible, score, -jnp.inf)
        bits = pltpu.bitcast(score, I32)
        keys_s[:, pl.ds(start, tk)] = jnp.where(bits < 0, bits ^ 0x7FFFFFFF, bits)
        return carry

    grouped(n_kb, score_block, 0)

    def pad_block(kb, carry):
        keys_s[:, pl.ds(pl.multiple_of(kb * tk, tk), tk)] = jnp.full((tq, tk), KEY_NEG_INF, I32)
        return carry

    lax.fori_loop(n_kb, (n_kb + sup - 1) // sup * sup, pad_block, 0)

    lane_pos = lax.broadcasted_iota(I32, (tq, LANE), 1)

    def count(pred):
        def body(kb, acc):
            for s in range(tk // LANE):
                first = pl.multiple_of(kb * tk + s * LANE, LANE)
                acc = acc + jnp.where(pred(keys_s[:, pl.ds(first, LANE)], first), 1.0, 0.0)
            return acc
        acc = grouped(n_kb, body, jnp.zeros((tq, LANE), F32))
        return jnp.broadcast_to(jnp.sum(acc, axis=1, keepdims=True), (tq, LANE))

    prefix_s[...] = jnp.full((tq, LANE), INT_MIN, I32)
    cnt_s[...] = jnp.full((tq, LANE), tk, F32) * n_kb.astype(F32)

    def select_cond(carry):
        it, settled = carry
        return jnp.logical_and(it < 32, settled == 0)

    def select_bit(carry):
        it, _ = carry
        bit = 31 - it
        prefix = prefix_s[...]
        cnt_old = cnt_s[...]
        settled = jnp.max(jnp.abs(cnt_old - topk)) == 0.0
        cand = jnp.where(bit == 31, jnp.zeros_like(prefix), prefix | jnp.left_shift(1, bit))
        cnt = count(lambda k, first: k >= cand)
        take = cnt >= topk
        prefix_s[...] = jnp.where(take, cand, prefix)
        cnt_s[...] = jnp.where(take, cnt, cnt_old)
        return it + 1, settled.astype(I32)

    lax.while_loop(select_cond, select_bit, (jnp.int32(0), jnp.int32(0)))
    kth = prefix_s[...]

    tied = jnp.where(kth > KEY_NEG_INF, cnt_s[...] - topk, 0.0)
    j_s[...] = jnp.full((tq, LANE), seq_len, I32)

    @pl.when(jnp.max(tied) > 0.0)
    def _break_ties():
        need = topk - count(lambda k, first: k > kth)

        def pos_bit(it, last):
            bit = (seq_len - 1).bit_length() - 1 - it
            cand = last | jnp.left_shift(1, bit)
            cnt = count(lambda k, first: jnp.where((first + lane_pos) < cand, k, INT_MIN) == kth)
            return jnp.where(cnt < need, cand, last)
        j_s[...] = lax.fori_loop(0, (seq_len - 1).bit_length(), pos_bit, jnp.zeros((tq, LANE), I32))

    last_tied = j_s[...]
    kth_eff = jnp.maximum(kth, KEY_NEG_INF + 1)

    log2e = 1.0 / math.log(2.0)
    m_s[...] = jnp.full_like(m_s, MASKED_LOGIT)
    l_s[...] = jnp.zeros_like(l_s)
    acc_s[...] = jnp.zeros_like(acc_s)
    n_sup = (n_kb + sup - 1) // sup

    def attend_super(sb, carry):
        base = pl.multiple_of(sb * (sup * tk), sup * tk)
        for t in range(sup * tk // LANE):
            first = pl.multiple_of(base + t * LANE, LANE)
            sel = keys_s[:, pl.ds(first, LANE)] >= jnp.where((first + lane_pos) > last_tied, kth_eff + 1, kth_eff)
            mask_s[:, t * LANE:(t + 1) * LANE] = jnp.where(sel, 0.0, -jnp.inf)
        def logits(h):
            hs = slice(h * HEAD_DIM, (h + 1) * HEAD_DIM)
            qh = qd_ref[:, hs]
            top = jnp.full((tq, LANE), -jnp.inf, F32)
            for j in range(sup):
                start = base + j * tk
                off = pl.multiple_of(jnp.clip(start - q0 + bias_shift, 0, bias_shift), LANE)
                s = (_dot_nt(qh, kd_ref[pl.ds(start, tk), hs]) * (scale * log2e)
                     + bias_s[h, :, pl.ds(off, tk)] + mask_s[:, j * tk:(j + 1) * tk])
                logit_s[h, :, j * tk:(j + 1) * tk] = s
                for t in range(tk // LANE):
                    top = jnp.maximum(top, s[:, t * LANE:(t + 1) * LANE])
            m_old = m_s[h]
            m_new = jnp.maximum(m_old, jnp.max(top, axis=1, keepdims=True))
            m_s[h] = m_new
            return m_new, jnp.exp2(m_old - m_new)

        def weigh(h, m_new, alpha):
            hs = slice(h * HEAD_DIM, (h + 1) * HEAD_DIM)
            pv = jnp.zeros((tq, HEAD_DIM), F32)
            psum = jnp.zeros((tq, LANE), F32)
            for j in range(sup):
                start = base + j * tk
                tiles = [jnp.exp2(logit_s[h, :, j * tk + t * LANE:j * tk + (t + 1) * LANE] - m_new)
                         for t in range(tk // LANE)]
                for p_t in tiles:
                    psum = psum + p_t
                p = jnp.concatenate([p_t.astype(BF16) for p_t in tiles], axis=1)
                pv = pv + _dot(p, vd_ref[pl.ds(start, tk), hs])
            acc_s[h] = alpha * acc_s[h] + pv
            l_s[h] = alpha * l_s[h] + psum

        nxt = logits(0)
        for h in range(DSA_HEADS):
            cur = nxt
            if h + 1 < DSA_HEADS:
                nxt = logits(h + 1)
            weigh(h, *cur)
        return carry

    lax.fori_loop(0, n_sup, attend_super, 0)
    for h in range(DSA_HEADS):
        denom = jnp.sum(l_s[h], axis=1, keepdims=True)
        o_ref[:, h * HEAD_DIM:(h + 1) * HEAD_DIM] = (acc_s[h] / denom).astype(o_ref.dtype)


def _dsa_attention(proj, side, rel_bias, batch, seq_len, cols):
    n = proj.shape[0]
    tq = min(DSA_TQ, seq_len)
    tk = min(DSA_TK, seq_len)
    nq = seq_len // tq
    topk = min(DSA_MAX_TOPK, seq_len // 4)
    bias_w = tk + 2 * tk
    sup = min(DSA_SUPER, seq_len // tk)
    kern = functools.partial(_dsa_kernel, tq=tq, tk=tk, sup=sup, topk=topk, seq_len=seq_len,
                             scale=HEAD_DIM ** -0.5)
    once = pl.Buffered(1)
    return pl.pallas_call(
        kern,
        grid=(batch, nq),
        in_specs=[
            pl.BlockSpec(memory_space=pltpu.SMEM),
            pl.BlockSpec((tq, DSA_WIDTH), lambda b, i: (b * nq + i, cols["dq"])),
            pl.BlockSpec((seq_len, DSA_WIDTH), lambda b, i: (b, cols["dk"]), pipeline_mode=once),
            pl.BlockSpec((seq_len, DSA_WIDTH), lambda b, i: (b, cols["dv"]), pipeline_mode=once),
            pl.BlockSpec((tq, IDX_HEADS * IDX_DIM), lambda b, i: (b * nq + i, cols["iq"])),
            pl.BlockSpec((seq_len, LANE), lambda b, i: (b, cols["ik_even"]), pipeline_mode=once),
            pl.BlockSpec((seq_len, LANE), lambda b, i: (b, cols["ik_odd"]), pipeline_mode=once),
            pl.BlockSpec((tq, LANE), lambda b, i: (b * nq + i, cols["iw"])),
        ],
        out_specs=pl.BlockSpec((tq, DSA_WIDTH), lambda b, i: (b * nq + i, 0)),
        out_shape=jax.ShapeDtypeStruct((n, DSA_WIDTH), BF16),
        scratch_shapes=[
            pltpu.VMEM((seq_len, LANE), BF16),
            pltpu.VMEM((seq_len, LANE), BF16),
            pltpu.VMEM((IDX_HEADS, tq, LANE), F32),
            pltpu.VMEM((tq, seq_len), I32),
            pltpu.VMEM((DSA_HEADS, tq, bias_w), F32),
            pltpu.VMEM((DSA_HEADS, tq, HEAD_DIM), F32),
            pltpu.VMEM((DSA_HEADS, tq, LANE), F32),
            pltpu.VMEM((DSA_HEADS, tq, LANE), F32),
            pltpu.VMEM((tq, LANE), I32),
            pltpu.VMEM((tq, LANE), I32),
            pltpu.VMEM((tq, LANE), F32),
            pltpu.VMEM((tq, sup * tk), F32),
            pltpu.VMEM((DSA_HEADS, tq, sup * tk), F32),
        ],
        compiler_params=_params(2),
        name="dsa_attention",
    )(rel_bias, proj, proj, proj, proj, side, side, side)


def _s5_operators(a_re, a_im, log_dt, b_re, b_im, c_re, c_im):
    hp = lax.Precision.HIGHEST
    L = SSM_CHUNK
    ar, ai = a_re.astype(F32), a_im.astype(F32)
    dt = jnp.exp(log_dt.astype(F32))[:, None]
    mag = jnp.exp(dt * ar)
    abar_re, abar_im = mag * jnp.cos(dt * ai), mag * jnp.sin(dt * ai)
    den = ar * ar + ai * ai
    nr = abar_re - 1.0
    f_re = (nr * ar + abar_im * ai) / den
    f_im = (abar_im * ar - nr * ai) / den
    br, bi = b_re.astype(F32), b_im.astype(F32)
    bb_re = f_re[..., None] * br - f_im[..., None] * bi
    bb_im = f_re[..., None] * bi + f_im[..., None] * br
    pw_re, pw_im = [jnp.ones_like(abar_re)], [jnp.zeros_like(abar_im)]
    for _ in range(L):
        pr, pi = pw_re[-1], pw_im[-1]
        pw_re.append(pr * abar_re - pi * abar_im)
        pw_im.append(pr * abar_im + pi * abar_re)
    pw_re, pw_im = jnp.stack(pw_re, 1), jnp.stack(pw_im, 1)
    cr, ci = c_re.astype(F32), c_im.astype(F32)
    cp_re = cr[:, None] * pw_re[:, :, None, :] - ci[:, None] * pw_im[:, :, None, :]
    cp_im = cr[:, None] * pw_im[:, :, None, :] + ci[:, None] * pw_re[:, :, None, :]
    kern = (jnp.einsum('gtop,gpi->gtoi', cp_re[:, :L], bb_re, precision=hp)
            - jnp.einsum('gtop,gpi->gtoi', cp_im[:, :L], bb_im, precision=hp))
    s_idx = jnp.arange(L)[:, None]
    t_idx = jnp.arange(L)[None, :]
    lag = t_idx - s_idx
    toep = jnp.where((lag >= 0)[None, :, :, None, None], kern[:, jnp.clip(lag, 0, L - 1)], 0.0)
    G = ar.shape[0]
    hc = SSM_GROUP
    mt = jnp.transpose(toep, (0, 1, 4, 2, 3)).reshape(G, L * hc, L * hc)
    rev_re, rev_im = pw_re[:, L - 1::-1][:, :L], pw_im[:, L - 1::-1][:, :L]
    w_re = (rev_re[:, :, None, :] * jnp.swapaxes(bb_re, 1, 2)[:, None]
            - rev_im[:, :, None, :] * jnp.swapaxes(bb_im, 1, 2)[:, None]).reshape(G, L * hc, SSM_STATE)
    w_im = (rev_re[:, :, None, :] * jnp.swapaxes(bb_im, 1, 2)[:, None]
            + rev_im[:, :, None, :] * jnp.swapaxes(bb_re, 1, 2)[:, None]).reshape(G, L * hc, SSM_STATE)
    v_re = jnp.transpose(cp_re[:, 1:], (0, 3, 1, 2)).reshape(G, SSM_STATE, L * hc)
    v_im = -jnp.transpose(cp_im[:, 1:], (0, 3, 1, 2)).reshape(G, SSM_STATE, L * hc)
    lam = jnp.stack([pw_re[:, L], pw_im[:, L]], axis=1)
    return mt, w_re, w_im, v_re, v_im, lam


def _to_chunks_kernel(*refs, hc):
    *u_refs, o_ref = refs
    nc = o_ref.shape[1]
    per_tile = LANE // hc
    for j, u_ref in enumerate(u_refs):
        steps = [u_ref[pl.ds(s, nc, stride=SSM_CHUNK), :].T for s in range(SSM_CHUNK)]
        for gl in range(per_tile):
            o_ref[j * per_tile + gl] = jnp.concatenate(
                [st[gl * hc:(gl + 1) * hc, :] for st in steps], axis=0).T


def _from_chunks_kernel(y_ref, o_ref, *tile_s, hc):
    nc = y_ref.shape[1]
    per_tile = LANE // hc
    for j, t_s in enumerate(tile_s):
        per_group = [y_ref[j * per_tile + gl].T for gl in range(per_tile)]
        for s in range(SSM_CHUNK):
            t_s[pl.ds(s, nc, stride=SSM_CHUNK), :] = jnp.concatenate(
                [yg[s * hc:(s + 1) * hc, :] for yg in per_group], axis=0).T
        o_ref[:, j * LANE:(j + 1) * LANE] = t_s[...]


def _s5_to_chunks(side, width, groups, rows):
    n = side.shape[0]
    nc = rows // SSM_CHUNK
    hc = width // groups
    tiles = width // LANE
    return pl.pallas_call(
        functools.partial(_to_chunks_kernel, hc=hc),
        grid=(n // rows,),
        in_specs=[pl.BlockSpec((rows, LANE), lambda i, j=j: (i, j)) for j in range(tiles)],
        out_specs=pl.BlockSpec((groups, nc, SSM_CHUNK * hc), lambda i: (0, i, 0)),
        out_shape=jax.ShapeDtypeStruct((groups, n // SSM_CHUNK, SSM_CHUNK * hc), F32),
        compiler_params=_params(1),
        name="s5_to_chunks",
    )(*([side] * tiles))


def _s5_from_chunks(y_chunks, rows):
    groups, n_chunks, cw = y_chunks.shape
    hc = cw // SSM_CHUNK
    width = groups * hc
    nc = rows // SSM_CHUNK
    return pl.pallas_call(
        functools.partial(_from_chunks_kernel, hc=hc),
        grid=(n_chunks // nc,),
        in_specs=[pl.BlockSpec((groups, nc, cw), lambda i: (0, i, 0))],
        out_specs=pl.BlockSpec((rows, width), lambda i: (i, 0)),
        out_shape=jax.ShapeDtypeStruct((n_chunks * SSM_CHUNK, width), F32),
        scratch_shapes=[pltpu.VMEM((rows, LANE), F32)] * (width // LANE),
        compiler_params=_params(1),
        name="s5_from_chunks",
    )(y_chunks)


def _s5_kernel(u_ref, mt_ref, wre_ref, wim_ref, vre_ref, vim_ref, lam_ref, y_ref,
               sre_s, sim_s, hre_s, him_s, *, batch):
    u = u_ref[...]
    sre_s[...] = _dot3(u, wre_ref[...])
    sim_s[...] = _dot3(u, wim_ref[...])
    lre = lam_ref[0:1, :]
    lim = lam_ref[1:2, :]
    n_chunks = u.shape[0] // batch
    state_w = lre.shape[1]

    def step(c, carry):
        out = []
        for b in range(batch):
            hr, hi = carry[b]
            row = b * n_chunks + c
            hre_s[pl.ds(row, 1), :] = hr
            him_s[pl.ds(row, 1), :] = hi
            sr = sre_s[pl.ds(row, 1), :]
            si = sim_s[pl.ds(row, 1), :]
            out.append((lre * hr - lim * hi + sr, lre * hi + lim * hr + si))
        return tuple(out)

    zero = jnp.zeros((1, state_w), F32)
    lax.fori_loop(0, n_chunks, step, tuple((zero, zero) for _ in range(batch)),
                  unroll=math.gcd(n_chunks, 8))
    y_ref[...] = (_dot3(u, mt_ref[...]) + _dot3(hre_s[...], vre_ref[...])
                  + _dot3(him_s[...], vim_ref[...]))


def _s5_scan(u_chunks, ops, batch):
    mt, w_re, w_im, v_re, v_im, lam = ops
    G, nc, cw = u_chunks.shape
    P = SSM_STATE
    grp = lambda *shape: pl.BlockSpec((None,) + shape, lambda g: (g,) + (0,) * len(shape))
    return pl.pallas_call(
        functools.partial(_s5_kernel, batch=batch),
        grid=(G,),
        in_specs=[grp(nc, cw), grp(cw, cw), grp(cw, P), grp(cw, P), grp(P, cw), grp(P, cw), grp(2, P)],
        out_specs=grp(nc, cw),
        out_shape=jax.ShapeDtypeStruct((G, nc, cw), F32),
        scratch_shapes=[pltpu.VMEM((nc, P), F32)] * 4,
        compiler_params=_params(1),
        name="s5_scan",
    )(u_chunks, mt, w_re, w_im, v_re, v_im, lam)


def _glu_kernel(y_ref, u_ref, d_ref, w_ref, b_ref, o_ref):
    y = y_ref[...] + d_ref[...] * u_ref[...]
    y = 0.5 * y * (1.0 + jnp.tanh(math.sqrt(2.0 / math.pi) * (y + 0.044715 * (y * y * y))))
    gate = jax.nn.sigmoid(_dot(y.astype(BF16), w_ref[...]) + b_ref[...])
    o_ref[...] = (y * gate).astype(o_ref.dtype)


def _s5_glu(y_scan, side, d_skip, w_glu, b_glu, tm):
    n, width = y_scan.shape
    row = lambda: pl.BlockSpec((1, width), lambda i: (0, 0))
    return pl.pallas_call(
        _glu_kernel,
        grid=(n // tm,),
        in_specs=[
            pl.BlockSpec((tm, width), lambda i: (i, 0)),
            pl.BlockSpec((tm, width), lambda i: (i, 0)),
            row(),
            pl.BlockSpec((width, width), lambda i: (0, 0)),
            row(),
        ],
        out_specs=pl.BlockSpec((tm, width), lambda i: (i, 0)),
        out_shape=jax.ShapeDtypeStruct((n, width), BF16),
        compiler_params=_params(1),
        name="s5_glu",
    )(y_scan, side, d_skip.reshape(1, width), w_glu.astype(BF16), b_glu.reshape(1, width))


def _outproj_kernel(sb_ref, ssm_ref, dsa_ref, w_ref, x_ref, gate_ref, g_ref, b_ref, sc_ref, sh_ref,
                    x_out_ref, u_out_ref, *, alpha):
    k0 = sb_ref.shape[1]
    k1 = k0 + ssm_ref.shape[1]
    mix = (_dot(sb_ref[...], w_ref[0:k0, :]) + _dot(ssm_ref[...], w_ref[k0:k1, :])
           + _dot(dsa_ref[...], w_ref[k1:, :]))
    y = alpha * x_ref[...] + (1.0 + gate_ref[...]) * mix
    x1 = _layer_norm(y, g_ref[...], b_ref[...])
    x_out_ref[...] = x1
    u_out_ref[...] = (x1 * (1.0 + sc_ref[...]) + sh_ref[...]).astype(u_out_ref.dtype)


def _outproj_ln(sb_o, ssm_o, dsa_o, w_out, layer, x, gate, ln_g, ln_b, scale, shift, seq_len, alpha, tm):
    n, d = x.shape
    per_batch = lambda: pl.BlockSpec((None, 1, d), lambda i: ((i * tm) // seq_len, 0, 0))
    vec = lambda: pl.BlockSpec((1, d), lambda i: (0, 0))
    tile = lambda w: pl.BlockSpec((tm, w), lambda i: (i, 0))
    return pl.pallas_call(
        functools.partial(_outproj_kernel, alpha=alpha),
        grid=(n // tm,),
        in_specs=[tile(sb_o.shape[1]), tile(ssm_o.shape[1]), tile(dsa_o.shape[1]),
                  pl.BlockSpec((None,) + w_out.shape[1:], lambda i: (layer, 0, 0)),
                  tile(d), per_batch(), vec(), vec(), per_batch(), per_batch()],
        out_specs=[tile(d), tile(d)],
        out_shape=[jax.ShapeDtypeStruct((n, d), F32), jax.ShapeDtypeStruct((n, d), BF16)],
        compiler_params=_params(1),
        name="outproj_ln",
    )(sb_o, ssm_o, dsa_o, w_out, x, gate, ln_g.reshape(1, d), ln_b.reshape(1, d), scale, shift)


def _ffn_up_kernel(u_ref, halo_ref, wg_ref, wv_ref, cwg_ref, cwv_ref, cbg_ref, cbv_ref, o_ref,
                   *, tm, halo, seq_len, sub):
    i = pl.program_id(0)
    has_prev = jnp.where((i * tm) % seq_len == 0, 0.0, 1.0)
    ux = jnp.concatenate([halo_ref[...] * has_prev.astype(BF16), u_ref[...]], axis=0)

    def conv(w_ref, cw_ref, cb_ref, cs):
        a = _dot(ux, w_ref[:, cs])
        out = cb_ref[:, cs] + cw_ref[CONV_WIDTH - 1:CONV_WIDTH, cs] * a[halo:, :]
        for back in range(1, CONV_WIDTH):
            tap = CONV_WIDTH - 1 - back
            out = out + cw_ref[tap:tap + 1, cs] * pltpu.roll(a, back, 0)[halo:, :]
        return out

    tf = o_ref.shape[1]
    for c in range(pl.cdiv(tf, sub)):
        cs = slice(c * sub, min((c + 1) * sub, tf))
        g = conv(wg_ref, cwg_ref, cbg_ref, cs)
        val = conv(wv_ref.at[0], cwv_ref, cbv_ref, cs)
        o_ref[:, cs] = (g * jax.nn.sigmoid(g) * val).astype(o_ref.dtype)


def _ffn_up(u, w_up, conv_w, conv_b, layer, d_ff, seq_len, tm, tf):
    n, d = u.shape
    ffp = conv_w.shape[2] // 2
    assert w_up.shape[2] == d_ff + ffp and d_ff % LANE == 0
    nj = ffp // tf
    halo = 16
    sub = 2 * LANE
    return pl.pallas_call(
        functools.partial(_ffn_up_kernel, tm=tm, halo=halo, seq_len=seq_len, sub=sub),
        grid=(n // tm, nj),
        in_specs=[
            pl.BlockSpec((tm, d), lambda i, j: (i, 0)),
            pl.BlockSpec((halo, d), lambda i, j: (jnp.maximum(i * (tm // halo) - 1, 0), 0)),
            pl.BlockSpec((None, d, tf), lambda i, j: (layer, 0, j)),
            pl.BlockSpec((pl.Element(1), pl.Element(d), pl.Element(tf)),
                         lambda i, j: (layer, 0, (j * (tf // LANE) + d_ff // LANE) * LANE)),
            pl.BlockSpec((None, CONV_WIDTH, tf), lambda i, j: (layer, 0, j)),
            pl.BlockSpec((None, CONV_WIDTH, tf), lambda i, j: (layer, 0, j + nj)),
            pl.BlockSpec((None, 1, tf), lambda i, j: (layer, 0, j)),
            pl.BlockSpec((None, 1, tf), lambda i, j: (layer, 0, j + nj)),
        ],
        out_specs=pl.BlockSpec((tm, tf), lambda i, j: (i, j)),
        out_shape=jax.ShapeDtypeStruct((n, ffp), BF16),
        compiler_params=_params(2),
        name="ffn_up_conv_gate",
    )(u, u, w_up, w_up, conv_w, conv_w, conv_b, conv_b)


def _ffn_down_kernel(h_ref, w_ref, x_ref, gate_ref, g_ref, b_ref, o_ref, *, alpha):
    y = alpha * x_ref[...] + (1.0 + gate_ref[...]) * _dot(h_ref[...], w_ref[...])
    o_ref[...] = _layer_norm(y, g_ref[...], b_ref[...])


def _ffn_down_ln(h, w_down, layer, x, gate, ln_g, ln_b, seq_len, alpha, tm):
    n, d = x.shape
    ffp = h.shape[1]
    return pl.pallas_call(
        functools.partial(_ffn_down_kernel, alpha=alpha),
        grid=(n // tm,),
        in_specs=[
            pl.BlockSpec((tm, ffp), lambda i: (i, 0)),
            pl.BlockSpec((None, ffp, d), lambda i: (layer, 0, 0), pipeline_mode=pl.Buffered(1)),
            pl.BlockSpec((tm, d), lambda i: (i, 0)),
            pl.BlockSpec((None, 1, d), lambda i: ((i * tm) // seq_len, 0, 0)),
            pl.BlockSpec((1, d), lambda i: (0, 0)),
            pl.BlockSpec((1, d), lambda i: (0, 0)),
        ],
        out_specs=pl.BlockSpec((tm, d), lambda i: (i, 0)),
        out_shape=jax.ShapeDtypeStruct((n, d), F32),
        compiler_params=_params(1),
        name="ffn_down_ln",
    )(h, w_down, x, gate, ln_g.reshape(1, d), ln_b.reshape(1, d))


def _round_up(v, m):
    return (v + m - 1) // m * m


def _split_w_in(w_in):
    w = w_in.astype(BF16)
    sizes = (SB_WIDTH, SB_WIDTH, SB_WIDTH, w.shape[2] - 3 * SB_WIDTH - 3 * DSA_WIDTH
             - IDX_HEADS * IDX_DIM - IDX_DIM - IDX_HEADS, DSA_WIDTH, DSA_WIDTH, DSA_WIDTH,
             IDX_HEADS * IDX_DIM, IDX_DIM, IDX_HEADS)
    offs = [0]
    for s in sizes:
        offs.append(offs[-1] + s)
    sbq, sbk, sbv, ssm, dq, dk, dv, iq, ik, iw = (w[:, :, offs[t]:offs[t + 1]] for t in range(10))
    z = lambda width: jnp.zeros(w.shape[:2] + (width,), BF16)
    main = jnp.concatenate([dq, dk, dv, sbq, sbk, sbv, ssm, iq], axis=2)
    side = jnp.concatenate([ssm, ik, z(LANE - IDX_DIM), z(LANE - IDX_DIM), ik, iw, z(LANE - IDX_HEADS)],
                           axis=2)
    return main, side, sizes[3]


def _pad_halves(a, d_ff, ffp):
    lead = a.shape[:-1]
    halves = a.reshape(lead + (2, d_ff))
    halves = jnp.pad(halves, [(0, 0)] * (len(lead) + 1) + [(0, ffp - d_ff)])
    return halves.reshape(lead + (2 * ffp,))


def kernel(x, c, w_ada, b_ada, w_in, w_out, ssm_a_re, ssm_a_im, ssm_log_dt, ssm_b_re, ssm_b_im,
           ssm_c_re, ssm_c_im, ssm_d, w_glu, b_glu, rel_bias, ln1_g, ln1_b, w_up, conv_w, conv_b,
           w_down, ln2_g, ln2_b):
    batch, seq_len, d_model = x.shape
    depth = w_ada.shape[0]
    n = batch * seq_len
    alpha = (2 * depth) ** 0.25
    d_ff = w_down.shape[1]
    ffp = _round_up(d_ff, 512)
    tm_proj = min(1024, seq_len)
    tm = min(512, seq_len)

    mod = _adaln_mod(c, w_ada, b_ada)
    w_main, w_side, ssm_width = _split_w_in(w_in)
    w_out_b = w_out.astype(BF16)
    w_up_p = jnp.pad(w_up.astype(BF16), ((0, 0), (0, 0), (0, ffp - d_ff)))
    conv_w_p = _pad_halves(conv_w, d_ff, ffp)
    conv_b_p = _pad_halves(conv_b[:, None, :], d_ff, ffp)
    w_down_p = jnp.pad(w_down.astype(BF16), ((0, 0), (0, ffp - d_ff), (0, 0)))
    relayout_rows = min(2048, seq_len)

    xf = x.reshape(n, d_model)
    for l in range(depth):
        sh_m, sc_m, g_m, sh_f, sc_f, g_f = (
            mod[l, :batch, t * d_model:(t + 1) * d_model].reshape(batch, 1, d_model) for t in range(N_MOD))
        proj = _modulated_matmul(xf, sc_m, sh_m, w_main, l, BF16, seq_len, tm_proj, 512, "in_proj")
        side = _modulated_matmul(xf, sc_m, sh_m, w_side, l, F32, seq_len, tm_proj, w_side.shape[2],
                                 "in_proj_side")

        sb_o = _sb_attention(proj, batch, seq_len, 3 * DSA_HEADS, 3 * DSA_HEADS + SB_HEADS,
                             3 * DSA_HEADS + 2 * SB_HEADS)
        side_blk = ssm_width // LANE
        dsa_cols = {"dq": 0, "dk": 1, "dv": 2,
                    "iq": (3 * DSA_WIDTH + 3 * SB_WIDTH + ssm_width) // (IDX_HEADS * IDX_DIM),
                    "ik_even": side_blk, "ik_odd": side_blk + 1, "iw": side_blk + 2}
        dsa_o = _dsa_attention(proj, side, rel_bias, batch, seq_len, dsa_cols)

        groups = ssm_width // SSM_GROUP
        u_chunks = _s5_to_chunks(side, ssm_width, groups, relayout_rows)
        ops = _s5_operators(ssm_a_re[l], ssm_a_im[l], ssm_log_dt[l], ssm_b_re[l], ssm_b_im[l],
                            ssm_c_re[l], ssm_c_im[l])
        y_chunks = _s5_scan(u_chunks, ops, batch)
        y_scan = _s5_from_chunks(y_chunks, relayout_rows)
        ssm_o = _s5_glu(y_scan, side, ssm_d[l], w_glu[l], b_glu[l], tm)

        x1, u_f = _outproj_ln(sb_o, ssm_o, dsa_o, w_out_b, l, xf, g_m, ln1_g[l], ln1_b[l],
                              sc_f, sh_f, seq_len, alpha, tm)
        h = _ffn_up(u_f, w_up_p, conv_w_p, conv_b_p, l, d_ff, seq_len, tm, 512)
        xf = _ffn_down_ln(h, w_down_p, l, x1, g_f, ln2_g[l], ln2_b[l], seq_len, alpha, tm)
    return xf.reshape(batch, seq_len, d_model)
```
